```python
import jax, jax.numpy as jnp
from jax import lax
import numpy as np

D_MODEL = 4096
BATCH = 2
SEQ = 4096
DEPTH = 2

CHUNK = 64
Q_BLOCK = 128
N_BRANCH = 3
BRANCH_WIDTH = D_MODEL // 4
IN_PARTS = 8
IN_WIDTH = IN_PARTS * BRANCH_WIDTH
SB_HEAD_DIM = 128
SB_HEADS = BRANCH_WIDTH // SB_HEAD_DIM
GM_SPAN = 128
GM_GROUP_DIM = 128
GM_GROUPS = BRANCH_WIDTH // GM_GROUP_DIM
CV_KSIZE = 3
CV_GROUPS = 8
N_EXPERTS = 64
TOP_K = 6
EXPERT_FF = 256
SHARED_FF = 1024
ROUTED_SCALE = 2.5
LN_EPS = 1e-5
DN_ALPHA = (2 * DEPTH) ** 0.25
DN_BETA = (8 * DEPTH) ** -0.25

kernel_name = "hybrid_sb_gmlp_conv_moe_deepnorm"


def layer_norm(x, g, b):
    xf = x.astype(jnp.float32)
    mu = jnp.mean(xf, axis=-1, keepdims=True)
    var = jnp.mean(jnp.square(xf - mu), axis=-1, keepdims=True)
    y = (xf - mu) * lax.rsqrt(var + LN_EPS)
    return (y * g.astype(jnp.float32) + b.astype(jnp.float32)).astype(x.dtype)


def stick_breaking_attention(q, k, v):
    S = q.shape[2]
    scale = SB_HEAD_DIM ** -0.5
    outs = []
    for blk in range(S // Q_BLOCK):
        q0 = blk * Q_BLOCK
        q1 = q0 + Q_BLOCK
        qb = q[:, :, q0:q1].astype(jnp.float32)
        kb = k[:, :, :q1].astype(jnp.float32)
        vb = v[:, :, :q1].astype(jnp.float32)
        z = jnp.einsum('bhtd,bhsd->bhts', qb, kb) * scale
        t_pos = q0 + jnp.arange(Q_BLOCK)[:, None]
        s_pos = jnp.arange(q1)[None, :]
        strict = s_pos < t_pos
        log_beta = jax.nn.log_sigmoid(z)
        log_one_minus = jnp.where(strict, log_beta - z, 0.0)
        tail = lax.cumsum(log_one_minus, axis=3, reverse=True) - log_one_minus
        a = jnp.where(strict, jnp.exp(log_beta + tail), 0.0)
        outs.append(jnp.einsum('bhts,bhsd->bhtd', a, vb))
    return jnp.concatenate(outs, axis=2).astype(v.dtype)


def spatial_gating(u, v, ln_g, ln_b, w_sp, b_sp):
    B, S, _ = u.shape
    u = jax.nn.gelu(u)
    v = layer_norm(jax.nn.gelu(v), ln_g, ln_b)
    vb = v.reshape(B, S // GM_SPAN, GM_SPAN, GM_GROUPS, GM_GROUP_DIM)
    cidx = jnp.arange(GM_SPAN) // CHUNK
    mask = cidx[:, None] >= cidx[None, :]
    w = jnp.where(mask[None], w_sp, 0.0)
    f = jnp.einsum('gts,bnsgc->bntgc', w, vb) + b_sp.T[None, None, :, :, None]
    return u * f.reshape(B, S, BRANCH_WIDTH)


def short_gated_conv(gb, gc, xh, conv_w):
    z = gc * xh
    y = lax.conv_general_dilated(
        z, conv_w[:, None, :].astype(z.dtype), window_strides=(1,),
        padding=[(CV_KSIZE - 1, 0)], dimension_numbers=('NWC', 'WIO', 'NWC'),
        feature_group_count=BRANCH_WIDTH)
    return gb * y


def mixer_sublayer(h, w_in, w_gate, b_gate, w_branch, w_out, gm_ln_g, gm_ln_b, w_sp, b_sp, conv_w):
    B, S, _ = h.shape
    proj = h @ w_in
    q, k, v, gu, gv, cb, cc, cx = jnp.split(proj, IN_PARTS, axis=-1)

    def heads(t):
        return t.reshape(B, S, SB_HEADS, SB_HEAD_DIM).transpose(0, 2, 1, 3)

    o_a = stick_breaking_attention(heads(q), heads(k), heads(v))
    o_a = o_a.transpose(0, 2, 1, 3).reshape(B, S, BRANCH_WIDTH)
    o_b = spatial_gating(gu, gv, gm_ln_g, gm_ln_b, w_sp, b_sp)
    o_c = short_gated_conv(cb, cc, cx, conv_w)

    branches = (o_a, o_b, o_c)
    merged = None
    for i in range(N_BRANCH):
        gate = jax.nn.sigmoid(h @ w_gate[i] + b_gate[i])
        term = gate * (branches[i] @ w_branch[i])
        merged = term if merged is None else merged + term
    return merged @ w_out


def moe_sublayer(h, w_router, b_router, w_e_gate, w_e_up, w_e_down, w_s_gate, w_s_up, w_s_down):
    B, S, D = h.shape
    t = h.reshape(B * S, D)
    scores = jax.nn.sigmoid((t @ w_router).astype(jnp.float32))
    _, idx = lax.top_k(scores + b_router.astype(jnp.float32), TOP_K)
    sel = jnp.take_along_axis(scores, idx, axis=-1)
    wts = sel / jnp.sum(sel, axis=-1, keepdims=True) * ROUTED_SCALE
    gate = jnp.sum(jax.nn.one_hot(idx, N_EXPERTS, dtype=jnp.float32) * wts[..., None], axis=1)
    hg = jnp.einsum('td,edf->tef', t, w_e_gate)
    hu = jnp.einsum('td,edf->tef', t, w_e_up)
    act = jax.nn.silu(hg) * hu * gate[..., None].astype(t.dtype)
    routed = jnp.einsum('tef,efd->td', act, w_e_down)
    shared = (jax.nn.silu(t @ w_s_gate) * (t @ w_s_up)) @ w_s_down
    return (routed + shared).reshape(B, S, D)


def setup_inputs(seed: int = 0) -> dict:
    key = jax.random.key(seed)
    ks = jax.random.split(key, 24)
    f32 = jnp.float32
    L, D = DEPTH, D_MODEL

    def nrm(k, shape, scale):
        return jax.random.normal(k, shape, f32) * scale

    return {
        "x": nrm(ks[0], (BATCH, SEQ, D), 1.0),
        "w_in": nrm(ks[1], (L, D, IN_WIDTH), D ** -0.5),
        "w_gate": nrm(ks[2], (L, N_BRANCH, D, D), D ** -0.5),
        "b_gate": nrm(ks[3], (L, N_BRANCH, D), 0.01),
        "w_branch": nrm(ks[4], (L, N_BRANCH, BRANCH_WIDTH, D), BRANCH_WIDTH ** -0.5 * DN_BETA),
        "w_out": nrm(ks[5], (L, D, D), D ** -0.5 * DN_BETA),
        "gm_ln_g": 1.0 + nrm(ks[6], (L, BRANCH_WIDTH), 0.01),
        "gm_ln_b": nrm(ks[7], (L, BRANCH_WIDTH), 0.01),
        "w_spatial": nrm(ks[8], (L, GM_GROUPS, GM_SPAN, GM_SPAN), GM_SPAN ** -0.5),
        "b_spatial": 1.0 + nrm(ks[9], (L, GM_GROUPS, GM_SPAN), 0.01),
        "conv_w": nrm(ks[10], (L, CV_KSIZE, BRANCH_WIDTH), CV_KSIZE ** -0.5),
        "ln1_g": 1.0 + nrm(ks[11], (L, D), 0.01),
        "ln1_b": nrm(ks[12], (L, D), 0.01),
        "w_router": nrm(ks[13], (L, D, N_EXPERTS), D ** -0.5),
        "b_router": nrm(ks[14], (L, N_EXPERTS), 0.01),
        "w_e_gate": nrm(ks[15], (L, N_EXPERTS, D, EXPERT_FF), D ** -0.5),
        "w_e_up": nrm(ks[16], (L, N_EXPERTS, D, EXPERT_FF), D ** -0.5),
        "w_e_down": nrm(ks[17], (L, N_EXPERTS, EXPERT_FF, D), EXPERT_FF ** -0.5 * DN_BETA),
        "w_s_gate": nrm(ks[18], (L, D, SHARED_FF), D ** -0.5),
        "w_s_up": nrm(ks[19], (L, D, SHARED_FF), D ** -0.5),
        "w_s_down": nrm(ks[20], (L, SHARED_FF, D), SHARED_FF ** -0.5 * DN_BETA),
        "ln2_g": 1.0 + nrm(ks[21], (L, D), 0.01),
        "ln2_b": nrm(ks[22], (L, D), 0.01),
    }


def reference(x, w_in, w_gate, b_gate, w_branch, w_out, gm_ln_g, gm_ln_b, w_spatial, b_spatial,
              conv_w, ln1_g, ln1_b, w_router, b_router, w_e_gate, w_e_up, w_e_down,
              w_s_gate, w_s_up, w_s_down, ln2_g, ln2_b):
    h = x
    for l in range(DEPTH):
        mix = mixer_sublayer(h, w_in[l], w_gate[l], b_gate[l], w_branch[l], w_out[l],
                             gm_ln_g[l], gm_ln_b[l], w_spatial[l], b_spatial[l], conv_w[l])
        h = layer_norm(DN_ALPHA * h + mix, ln1_g[l], ln1_b[l])
        ffn = moe_sublayer(h, w_router[l], b_router[l], w_e_gate[l], w_e_up[l], w_e_down[l],
                           w_s_gate[l], w_s_up[l], w_s_down[l])
        h = layer_norm(DN_ALPHA * h + ffn, ln2_g[l], ln2_b[l])
    return h
```

```python
import functools

import jax
import jax.numpy as jnp
from jax import lax
from jax.experimental import pallas as pl
from jax.experimental.pallas import tpu as pltpu

F32 = jnp.float32
BF16 = jnp.bfloat16
I32 = jnp.int32

CHUNK = 64
HEAD_DIM = 128
GM_SPAN = 128
GM_GROUP_DIM = 128
IN_PARTS = 8
N_BRANCH = 3
TOP_K = 6
ROUTED_SCALE = 2.5
LN_EPS = 1e-5

V7X_VMEM_LIMIT_BYTES = 56 * 1024 * 1024
LANES = 128
SUBLANES = 8
CAST_ROWS = 256
EXPERT_ROW_TILE = 256
IDX_LANES = 8


def _params(sem):
    return pltpu.CompilerParams(dimension_semantics=sem, vmem_limit_bytes=V7X_VMEM_LIMIT_BYTES)


def _cast_rows_to_bf16(src_ref, dst_ref):
    k = src_ref.shape[0]
    rows = min(CAST_ROWS, k)

    def body(i, c):
        r = pl.multiple_of(i * rows, rows)
        dst_ref[pl.ds(r, rows), :] = src_ref[pl.ds(r, rows), :].astype(BF16)
        return c

    lax.fori_loop(0, k // rows, body, 0)


def _dot(a, b):
    return jnp.dot(a, b, preferred_element_type=F32)


PACK_COLS = 2 * LANES
HIGH_HALF = -65536


def _pack_bf16_pairs(lo, hi):
    lo_bits = lax.bitcast_convert_type(lo.astype(BF16).astype(F32), I32)
    hi_bits = lax.bitcast_convert_type(hi.astype(BF16).astype(F32), I32)
    return hi_bits | lax.shift_right_logical(lo_bits, 16)


def _unpack_bf16_pairs(word):
    lo = lax.bitcast_convert_type(word << 16, F32)
    hi = lax.bitcast_convert_type(word & HIGH_HALF, F32)
    return lo, hi


def _store_packed_rows(values, pk_ref):
    for j in range(pk_ref.shape[1]):
        c0 = j * PACK_COLS
        pk_ref[:, j, :] = _pack_bf16_pairs(values[:, c0:c0 + LANES], values[:, c0 + LANES:c0 + PACK_COLS])


def _mm_kernel(a_ref, w_ref, o_ref, wbf_ref):
    @pl.when(pl.program_id(1) == 0)
    def _():
        _cast_rows_to_bf16(w_ref, wbf_ref)

    o_ref[...] = _dot(a_ref[...], wbf_ref[...]).astype(o_ref.dtype)


def _matmul(a, w, layer, col_off, n_cols, tm, tn, out_dtype, name):
    m, k = a.shape
    tm = min(tm, m)
    while n_cols % tn or col_off % tn:
        tn //= 2
    col_block_off = col_off // tn
    return pl.pallas_call(
        _mm_kernel,
        grid=(n_cols // tn, m // tm),
        in_specs=[
            pl.BlockSpec((tm, k), lambda n, i: (i, 0)),
            pl.BlockSpec((None, k, tn), lambda n, i: (layer, 0, n + col_block_off),
                         pipeline_mode=pl.Buffered(1)),
        ],
        out_specs=pl.BlockSpec((tm, tn), lambda n, i: (i, n)),
        out_shape=jax.ShapeDtypeStruct((m, n_cols), out_dtype),
        scratch_shapes=[pltpu.VMEM((k, tn), BF16)],
        compiler_params=_params(("arbitrary", "arbitrary")),
        name=name,
    )(a, w)


def _attn_kernel(q_ref, k_ref, v_ref, o_ref, *, tq, kb, scale):
    qi = pl.program_id(2)
    q = q_ref[...]
    n_key_blocks = (qi + 1) * (tq // kb)
    t_pos = lax.broadcasted_iota(I32, (tq, kb), 0) + qi * tq
    s_loc = lax.broadcasted_iota(I32, (tq, kb), 1)
    r = lax.broadcasted_iota(I32, (2 * kb, 2 * kb), 0)
    c = lax.broadcasted_iota(I32, (2 * kb, 2 * kb), 1)
    r = jnp.where(r >= kb, r - kb, r)
    suffix = jnp.where((c >= kb) | (r > c), 1.0, 0.0).astype(BF16)

    def body(step, carry):
        acc, later_sum = carry
        j = n_key_blocks - 1 - step
        k0 = pl.multiple_of(j * kb, kb)
        kj = k_ref[pl.ds(k0, kb), :]
        vj = v_ref[pl.ds(k0, kb), :]
        z = lax.dot_general(q, kj, (((1,), (1,)), ((), ())), preferred_element_type=F32) * scale
        log_beta = jnp.minimum(z, 0.0) - jnp.log1p(jnp.exp(-jnp.abs(z)))
        strict = (s_loc + k0) < t_pos
        log_one_minus = jnp.where(strict, log_beta - z, 0.0)
        hi = log_one_minus.astype(BF16)
        lo = (log_one_minus - hi.astype(F32)).astype(BF16)
        sums = _dot(jnp.concatenate([hi, lo], axis=1), suffix)
        tail = sums[:, :kb] + later_sum
        a = jnp.where(strict, jnp.exp(log_beta + tail), 0.0)
        acc = acc + _dot(a.astype(BF16), vj)
        return acc, later_sum + sums[:, kb:]

    zeros = jnp.zeros((tq, kb), F32)
    acc, _ = lax.fori_loop(0, n_key_blocks, body, (jnp.zeros((tq, HEAD_DIM), F32), zeros))
    o_ref[...] = acc.astype(o_ref.dtype)


def _attention(qkv, batch, seq, heads, tq):
    m = qkv.shape[0]
    tq = min(tq, seq)
    kb = HEAD_DIM
    nq = seq // tq
    kern = functools.partial(_attn_kernel, tq=tq, kb=kb, scale=HEAD_DIM ** -0.5)
    return pl.pallas_call(
        kern,
        grid=(batch, heads, nq),
        in_specs=[
            pl.BlockSpec((tq, HEAD_DIM), lambda b, h, i: (b * nq + i, h)),
            pl.BlockSpec((seq, HEAD_DIM), lambda b, h, i: (b, heads + h)),
            pl.BlockSpec((seq, HEAD_DIM), lambda b, h, i: (b, 2 * heads + h)),
        ],
        out_specs=pl.BlockSpec((tq, HEAD_DIM), lambda b, h, i: (b * nq + i, h)),
        out_shape=jax.ShapeDtypeStruct((m, heads * HEAD_DIM), BF16),
        compiler_params=_params(("arbitrary", "arbitrary", "arbitrary")),
        name="sb_attention",
    )(qkv, qkv, qkv)


def _gelu_tanh(x):
    return 0.5 * x * (1.0 + jnp.tanh(0.7978845608028654 * (x + 0.044715 * (x * x * x))))


def _gmlp_conv_kernel(gu_ref, gv_ref, cb_ref, cc_ref, cx_ref, ccp_ref, cxp_ref, lng_ref, lnb_ref,
                      wsp_ref, bsp_ref, cw_ref, o_ref, *, tt, seq, groups):
    i = pl.program_id(0)
    z = cc_ref[...] * cx_ref[...]
    first = (i * tt) % seq == 0
    zp = jnp.where(first, 0.0, ccp_ref[...] * cxp_ref[...])
    row = lax.broadcasted_iota(I32, z.shape, 0)
    z1 = jnp.where(row == 0, zp[SUBLANES - 1:SUBLANES, :], pltpu.roll(z, 1, 0))
    z2 = pltpu.roll(z, 2, 0)
    z2 = jnp.where(row == 0, zp[SUBLANES - 2:SUBLANES - 1, :], z2)
    z2 = jnp.where(row == 1, zp[SUBLANES - 1:SUBLANES, :], z2)
    y = cw_ref[0:1, :] * z2 + cw_ref[1:2, :] * z1 + cw_ref[2:3, :] * z
    o_ref[1] = (cb_ref[...] * y).astype(o_ref.dtype)

    v = _gelu_tanh(gv_ref[...])
    mu = jnp.mean(v, axis=-1, keepdims=True)
    var = jnp.mean(jnp.square(v - mu), axis=-1, keepdims=True)
    vn = ((v - mu) * lax.rsqrt(var + LN_EPS) * lng_ref[...] + lnb_ref[...]).astype(BF16)
    u = _gelu_tanh(gu_ref[...])
    tr = lax.broadcasted_iota(I32, (GM_SPAN, GM_SPAN), 0) // CHUNK
    sc = lax.broadcasted_iota(I32, (GM_SPAN, GM_SPAN), 1) // CHUNK
    causal = tr >= sc
    for g in range(groups):
        wm = jnp.where(causal, wsp_ref[g], 0.0).astype(BF16)
        cols = slice(g * GM_GROUP_DIM, (g + 1) * GM_GROUP_DIM)
        for s in range(tt // GM_SPAN):
            rows = slice(s * GM_SPAN, (s + 1) * GM_SPAN)
            f = _dot(wm, vn[rows, cols]) + bsp_ref[g]
            o_ref[0, rows, cols] = (u[rows, cols] * f).astype(o_ref.dtype)


def _gmlp_conv(rest, gm_ln_g, gm_ln_b, w_spatial, b_spatial_bcast, conv_w, layer, seq, bw, tt):
    m = rest.shape[0]
    tt = min(tt, seq)
    groups = bw // GM_GROUP_DIM
    halo_blocks = tt // SUBLANES
    kern = functools.partial(_gmlp_conv_kernel, tt=tt, seq=seq, groups=groups)

    def part(p):
        return pl.BlockSpec((tt, bw), lambda i: (i, p))

    def halo(p):
        return pl.BlockSpec((SUBLANES, bw), lambda i: (jnp.maximum(i * halo_blocks - 1, 0), p))

    return pl.pallas_call(
        kern,
        grid=(m // tt,),
        in_specs=[
            part(0), part(1), part(2), part(3), part(4), halo(3), halo(4),
            pl.BlockSpec((None, 1, bw), lambda i: (layer, 0, 0)),
            pl.BlockSpec((None, 1, bw), lambda i: (layer, 0, 0)),
            pl.BlockSpec((None, groups, GM_SPAN, GM_SPAN), lambda i: (layer, 0, 0, 0)),
            pl.BlockSpec((None, groups, GM_SPAN, GM_SPAN), lambda i: (layer, 0, 0, 0)),
            pl.BlockSpec((None, 3, bw), lambda i: (layer, 0, 0)),
        ],
        out_specs=pl.BlockSpec((2, tt, bw), lambda i: (0, i, 0)),
        out_shape=jax.ShapeDtypeStruct((2, m, bw), BF16),
        compiler_params=_params(("arbitrary",)),
        name="gmlp_conv",
    )(rest, rest, rest, rest, rest, rest, rest, gm_ln_g, gm_ln_b, w_spatial, b_spatial_bcast, conv_w)


def _merge_kernel(h_ref, oa_ref, obc_ref, wg_ref, bg_ref, wb_ref, o_ref, wg_bf, wb_bf):
    @pl.when(pl.program_id(1) == 0)
    def _():
        for i in range(N_BRANCH):
            _cast_rows_to_bf16(wg_ref.at[i], wg_bf.at[i])
            _cast_rows_to_bf16(wb_ref.at[i], wb_bf.at[i])

    h = h_ref[...]
    merged = None
    for i in range(N_BRANCH):
        gate = jax.nn.sigmoid(_dot(h, wg_bf[i]) + bg_ref[i])
        branch = oa_ref[...] if i == 0 else obc_ref[i - 1]
        term = gate * _dot(branch, wb_bf[i])
        merged = term if merged is None else merged + term
    o_ref[...] = merged.astype(o_ref.dtype)


def _merge(h_bf, o_a, o_bc, w_gate, b_gate4, w_branch, layer, tm, tn):
    m, d = h_bf.shape
    bw = o_a.shape[1]
    tm = min(tm, m)
    tn = min(tn, d)
    return pl.pallas_call(
        _merge_kernel,
        grid=(d // tn, m // tm),
        in_specs=[
            pl.BlockSpec((tm, d), lambda n, i: (i, 0)),
            pl.BlockSpec((tm, bw), lambda n, i: (i, 0)),
            pl.BlockSpec((2, tm, bw), lambda n, i: (0, i, 0)),
            pl.BlockSpec((None, N_BRANCH, d, tn), lambda n, i: (layer, 0, 0, n),
                         pipeline_mode=pl.Buffered(1)),
            pl.BlockSpec((None, N_BRANCH, 1, tn), lambda n, i: (layer, 0, 0, n)),
            pl.BlockSpec((None, N_BRANCH, bw, tn), lambda n, i: (layer, 0, 0, n),
                         pipeline_mode=pl.Buffered(1)),
        ],
        out_specs=pl.BlockSpec((tm, tn), lambda n, i: (i, n)),
        out_shape=jax.ShapeDtypeStruct((m, d), BF16),
        scratch_shapes=[pltpu.VMEM((N_BRANCH, d, tn), BF16), pltpu.VMEM((N_BRANCH, bw, tn), BF16)],
        compiler_params=_params(("arbitrary", "arbitrary")),
        name="branch_merge",
    )(h_bf, o_a, o_bc, w_gate, b_gate4, w_branch)


def _layer_norm_rows(y, g, b):
    mu = jnp.mean(y, axis=-1, keepdims=True)
    var = jnp.mean(jnp.square(y - mu), axis=-1, keepdims=True)
    return (y - mu) * lax.rsqrt(var + LN_EPS) * g + b


def _residual_ln_kernel(h_ref, f_ref, g_ref, b_ref, o_ref, obf_ref, opk_ref, *, alpha):
    out = _layer_norm_rows(alpha * h_ref[...] + f_ref[...], g_ref[...], b_ref[...])
    o_ref[...] = out
    obf_ref[...] = out.astype(BF16)
    _store_packed_rows(out, opk_ref)


def _residual_ln(h, f, g, b, layer, alpha, tm):
    m, d = h.shape
    tm = min(tm, m)
    row = pl.BlockSpec((tm, d), lambda i: (i, 0))
    vec = pl.BlockSpec((None, 1, d), lambda i: (layer, 0, 0))
    packed = pl.BlockSpec((tm, d // PACK_COLS, LANES), lambda i: (i, 0, 0))
    return pl.pallas_call(
        functools.partial(_residual_ln_kernel, alpha=alpha),
        grid=(m // tm,),
        in_specs=[row, row, vec, vec],
        out_specs=[row, row, packed],
        out_shape=[jax.ShapeDtypeStruct((m, d), F32), jax.ShapeDtypeStruct((m, d), BF16),
                   jax.ShapeDtypeStruct((m, d // PACK_COLS, LANES), I32)],
        compiler_params=_params(("arbitrary",)),
        name="residual_ln",
    )(h, f, g, b)


def _router_kernel(h_ref, w_ref, b_ref, idx_ref, rank_ref, wts_ref, cnt_ref, run_ref, *, tr, n_exp):
    @pl.when(pl.program_id(0) == 0)
    def _():
        run_ref[...] = jnp.zeros_like(run_ref)

    h = h_ref[...]
    w = w_ref[...]
    h_hi = h.astype(BF16)
    h_lo = (h - h_hi.astype(F32)).astype(BF16)
    w_hi = w.astype(BF16)
    w_lo = (w - w_hi.astype(F32)).astype(BF16)
    logits = _dot(h_hi, w_hi) + (_dot(h_hi, w_lo) + _dot(h_lo, w_hi))
    scores = jax.nn.sigmoid(logits)
    sel = scores + b_ref[...]
    lane = lax.broadcasted_iota(I32, (tr, n_exp), 1).astype(F32)
    out_lane = lax.broadcasted_iota(I32, (tr, IDX_LANES), 1)
    mask = jnp.zeros((tr, n_exp), F32)
    idx_out = jnp.zeros((tr, IDX_LANES), F32)
    picks = []
    for k in range(TOP_K):
        best = jnp.max(sel, axis=-1, keepdims=True)
        pick = jnp.min(jnp.where(sel == best, lane, float(n_exp)), axis=-1, keepdims=True)
        hit = lane == pick
        mask = jnp.where(hit, 1.0, mask)
        sel = jnp.where(hit, -jnp.inf, sel)
        idx_out = jnp.where(out_lane == k, pick, idx_out)
        picks.append(hit)
    chosen_scores = mask * scores
    gates = chosen_scores / jnp.sum(chosen_scores, axis=-1, keepdims=True) * ROUTED_SCALE

    rr = lax.broadcasted_iota(I32, (tr, tr), 0)
    cc = lax.broadcasted_iota(I32, (tr, tr), 1)
    lower = jnp.where(rr > cc, 1.0, 0.0).astype(BF16)
    rank = _dot(lower, mask.astype(BF16)) + run_ref[...]
    run_ref[...] = run_ref[...] + jnp.sum(mask, axis=0, keepdims=True)
    cnt_ref[...] = run_ref[...].astype(I32)

    rank_out = jnp.zeros((tr, IDX_LANES), F32)
    wts_out = jnp.zeros((tr, IDX_LANES), F32)
    for k in range(TOP_K):
        rank_out = jnp.where(out_lane == k, jnp.sum(jnp.where(picks[k], rank, 0.0), axis=-1, keepdims=True), rank_out)
        wts_out = jnp.where(out_lane == k, jnp.sum(jnp.where(picks[k], gates, 0.0), axis=-1, keepdims=True), wts_out)
    idx_ref[...] = idx_out.astype(I32)
    rank_ref[...] = rank_out.astype(I32)
    wts_ref[...] = wts_out


def _router(h, w_router, b_router3, layer, tr):
    m, d = h.shape
    n_exp = w_router.shape[-1]
    tr = min(tr, m)
    lists = pl.BlockSpec((tr, IDX_LANES), lambda i: (i, 0))
    return pl.pallas_call(
        functools.partial(_router_kernel, tr=tr, n_exp=n_exp),
        grid=(m // tr,),
        in_specs=[
            pl.BlockSpec((tr, d), lambda i: (i, 0)),
            pl.BlockSpec((None, d, n_exp), lambda i: (layer, 0, 0)),
            pl.BlockSpec((None, 1, n_exp), lambda i: (layer, 0, 0)),
        ],
        out_specs=[lists, lists, lists, pl.BlockSpec((1, n_exp), lambda i: (0, 0))],
        out_shape=[
            jax.ShapeDtypeStruct((m, IDX_LANES), I32),
            jax.ShapeDtypeStruct((m, IDX_LANES), I32),
            jax.ShapeDtypeStruct((m, IDX_LANES), F32),
            jax.ShapeDtypeStruct((1, n_exp), I32),
        ],
        scratch_shapes=[pltpu.VMEM((1, n_exp), F32)],
        compiler_params=_params(("arbitrary",)),
        name="router_topk",
    )(h, w_router, b_router3)


def _inverse_kernel(dest_ref, src_ref, *, n_pairs, n_rows):
    def clear(r, c):
        src_ref[r] = 0
        return c

    lax.fori_loop(0, n_rows, clear, 0, unroll=8)

    def place(t, c):
        for k in range(TOP_K):
            src_ref[dest_ref[t * TOP_K + k]] = t
        return c

    lax.fori_loop(0, n_pairs // TOP_K, place, 0, unroll=2)


def _inverse_map(dest_flat, n_rows):
    n_pairs = dest_flat.shape[0]
    return pl.pallas_call(
        functools.partial(_inverse_kernel, n_pairs=n_pairs, n_rows=n_rows),
        in_specs=[pl.BlockSpec(memory_space=pltpu.SMEM)],
        out_specs=pl.BlockSpec(memory_space=pltpu.SMEM),
        out_shape=jax.ShapeDtypeStruct((n_rows,), I32),
        name="inverse_map",
    )(dest_flat)


def _expert_kernel(te_ref, nu_ref, src_ref, h_hbm, wg_ref, wu_ref, wd_ref, o_ref,
                   xbuf, x_bf, wg_bf, wu_bf, wd_bf, sem, *, tile):
    i = pl.program_id(0)
    n_used = nu_ref[0]
    slot = i % 2

    def start_gather(tile_idx, s):
        def body(r, c):
            row = src_ref[tile_idx * tile + r]
            pltpu.make_async_copy(h_hbm.at[pl.ds(row, 1)], xbuf.at[s, pl.ds(r, 1)], sem.at[s]).start()
            return c
        lax.fori_loop(0, tile, body, 0, unroll=8)

    def wait_gather(s):
        pltpu.make_async_copy(h_hbm.at[pl.ds(0, tile)], xbuf.at[s], sem.at[s]).wait()

    @pl.when(i == 0)
    def _():
        start_gather(0, 0)

    @pl.when(i + 1 < n_used)
    def _():
        start_gather(i + 1, 1 - slot)

    @pl.when(i < n_used)
    def _():
        changed = jnp.logical_or(i == 0, te_ref[i] != te_ref[jnp.maximum(i - 1, 0)])

        @pl.when(changed)
        def _():
            _cast_rows_to_bf16(wg_ref, wg_bf)
            _cast_rows_to_bf16(wu_ref, wu_bf)
            _cast_rows_to_bf16(wd_ref, wd_bf)

        wait_gather(slot)
        for j in range(xbuf.shape[2]):
            lo, hi = _unpack_bf16_pairs(xbuf[slot, :, j, :])
            c0 = j * PACK_COLS
            x_bf[:, c0:c0 + LANES] = lo.astype(BF16)
            x_bf[:, c0 + LANES:c0 + PACK_COLS] = hi.astype(BF16)
        x = x_bf[...]
        act = jax.nn.silu(_dot(x, wg_bf[...])) * _dot(x, wu_bf[...])
        _store_packed_rows(_dot(act.astype(BF16), wd_bf[...]), o_ref)

    @pl.when(i >= n_used)
    def _():
        o_ref[...] = jnp.zeros_like(o_ref)


def _experts(tile_expert, n_used, src, h_pk, w_e_gate, w_e_up, w_e_down, layer, n_tiles, tile):
    m, packs, _ = h_pk.shape
    d = packs * PACK_COLS
    ff = w_e_gate.shape[-1]

    grid_spec = pltpu.PrefetchScalarGridSpec(
        num_scalar_prefetch=3,
        grid=(n_tiles,),
        in_specs=[
            pl.BlockSpec(memory_space=pl.ANY),
            pl.BlockSpec((None, None, d, ff), lambda i, te, nu, s: (layer, te[i], 0, 0)),
            pl.BlockSpec((None, None, d, ff), lambda i, te, nu, s: (layer, te[i], 0, 0)),
            pl.BlockSpec((None, None, ff, d), lambda i, te, nu, s: (layer, te[i], 0, 0)),
        ],
        out_specs=pl.BlockSpec((tile, packs, LANES), lambda i, te, nu, s: (i, 0, 0)),
        scratch_shapes=[
            pltpu.VMEM((2, tile, packs, LANES), I32),
            pltpu.VMEM((tile, d), BF16),
            pltpu.VMEM((d, ff), BF16),
            pltpu.VMEM((d, ff), BF16),
            pltpu.VMEM((ff, d), BF16),
            pltpu.SemaphoreType.DMA((2,)),
        ],
    )
    return pl.pallas_call(
        functools.partial(_expert_kernel, tile=tile),
        grid_spec=grid_spec,
        out_shape=jax.ShapeDtypeStruct((n_tiles * tile, packs, LANES), I32),
        compiler_params=_params(("arbitrary",)),
        name="expert_mlp",
    )(tile_expert, n_used, src, h_pk, w_e_gate, w_e_up, w_e_down)


def _swiglu_kernel(a_ref, wg_ref, wu_ref, o_ref, wg_bf, wu_bf):
    @pl.when(pl.program_id(1) == 0)
    def _():
        _cast_rows_to_bf16(wg_ref, wg_bf)
        _cast_rows_to_bf16(wu_ref, wu_bf)

    a = a_ref[...]
    o_ref[...] = (jax.nn.silu(_dot(a, wg_bf[...])) * _dot(a, wu_bf[...])).astype(o_ref.dtype)


def _swiglu(a, w_gate, w_up, layer, tm, tn):
    m, k = a.shape
    n = w_gate.shape[-1]
    tm = min(tm, m)
    tn = min(tn, n)
    wspec = pl.BlockSpec((None, k, tn), lambda j, i: (layer, 0, j), pipeline_mode=pl.Buffered(1))
    return pl.pallas_call(
        _swiglu_kernel,
        grid=(n // tn, m // tm),
        in_specs=[pl.BlockSpec((tm, k), lambda j, i: (i, 0)), wspec, wspec],
        out_specs=pl.BlockSpec((tm, tn), lambda j, i: (i, j)),
        out_shape=jax.ShapeDtypeStruct((m, n), BF16),
        scratch_shapes=[pltpu.VMEM((k, tn), BF16), pltpu.VMEM((k, tn), BF16)],
        compiler_params=_params(("arbitrary", "arbitrary")),
        name="shared_swiglu",
    )(a, w_gate, w_up)


def _combine_kernel(dest_ref, y_hbm, h_ref, sh_ref, wts_ref, g_ref, b_ref, o_ref, obf_ref,
                    ybuf, routed, sem, *, tc, alpha):
    i = pl.program_id(0)
    n_steps = pl.num_programs(0)
    slot = i % 2

    def start_gather(step, s):
        def body(t, c):
            for k in range(TOP_K):
                row = dest_ref[(step * tc + t) * TOP_K + k]
                pltpu.make_async_copy(y_hbm.at[pl.ds(row, 1)], ybuf.at[s, k, pl.ds(t, 1)], sem.at[s]).start()
            return c
        lax.fori_loop(0, tc, body, 0, unroll=2)

    def wait_gather(s):
        for k in range(TOP_K):
            pltpu.make_async_copy(y_hbm.at[pl.ds(0, tc)], ybuf.at[s, k], sem.at[s]).wait()

    @pl.when(i == 0)
    def _():
        start_gather(0, 0)

    @pl.when(i + 1 < n_steps)
    def _():
        start_gather(i + 1, 1 - slot)

    wait_gather(slot)
    wts = wts_ref[...]
    for j in range(ybuf.shape[3]):
        acc_lo = None
        for k in range(TOP_K):
            lo, hi = _unpack_bf16_pairs(ybuf[slot, k, :, j, :])
            w = wts[:, k:k + 1]
            acc_lo = w * lo if acc_lo is None else acc_lo + w * lo
            acc_hi = w * hi if k == 0 else acc_hi + w * hi
        c0 = j * PACK_COLS
        routed[:, c0:c0 + LANES] = acc_lo
        routed[:, c0 + LANES:c0 + PACK_COLS] = acc_hi
    y = alpha * h_ref[...] + (routed[...] + sh_ref[...])
    out = _layer_norm_rows(y, g_ref[...], b_ref[...])
    o_ref[...] = out
    obf_ref[...] = out.astype(BF16)


def _combine(dest_flat, y_sorted, h, shared, wts, g, b, layer, alpha, tc):
    m, d = h.shape
    tc = min(tc, m)
    row = pl.BlockSpec((tc, d), lambda i, dest: (i, 0))
    vec = pl.BlockSpec((None, 1, d), lambda i, dest: (layer, 0, 0))
    grid_spec = pltpu.PrefetchScalarGridSpec(
        num_scalar_prefetch=1,
        grid=(m // tc,),
        in_specs=[
            pl.BlockSpec(memory_space=pl.ANY),
            row, row,
            pl.BlockSpec((tc, IDX_LANES), lambda i, dest: (i, 0)),
            vec, vec,
        ],
        out_specs=[row, row],
        scratch_shapes=[pltpu.VMEM((2, TOP_K, tc, d // PACK_COLS, LANES), I32),
                        pltpu.VMEM((tc, d), F32), pltpu.SemaphoreType.DMA((2,))],
    )
    return pl.pallas_call(
        functools.partial(_combine_kernel, tc=tc, alpha=alpha),
        grid_spec=grid_spec,
        out_shape=[jax.ShapeDtypeStruct((m, d), F32), jax.ShapeDtypeStruct((m, d), BF16)],
        compiler_params=_params(("arbitrary",)),
        name="moe_combine_ln",
    )(dest_flat, y_sorted, h, shared, wts, g, b)


def kernel(x, w_in, w_gate, b_gate, w_branch, w_out, gm_ln_g, gm_ln_b, w_spatial, b_spatial, conv_w,
           ln1_g, ln1_b, w_router, b_router, w_e_gate, w_e_up, w_e_down, w_s_gate, w_s_up, w_s_down,
           ln2_g, ln2_b):
    batch, seq, d = x.shape
    depth = w_in.shape[0]
    m = batch * seq
    bw = d // 4
    heads = bw // HEAD_DIM
    n_exp = w_router.shape[-1]
    alpha = float((2 * depth) ** 0.25)
    tile = EXPERT_ROW_TILE
    n_tiles = (m * TOP_K + n_exp * (tile - 1)) // tile + 1
    n_rows = n_tiles * tile

    b_gate4 = b_gate.reshape(depth, N_BRANCH, 1, d)
    gm_g3 = gm_ln_g.reshape(depth, 1, bw)
    gm_b3 = gm_ln_b.reshape(depth, 1, bw)
    b_sp_bcast = jnp.broadcast_to(b_spatial[..., None], b_spatial.shape + (GM_SPAN,))
    ln1_g3, ln1_b3 = ln1_g.reshape(depth, 1, d), ln1_b.reshape(depth, 1, d)
    ln2_g3, ln2_b3 = ln2_g.reshape(depth, 1, d), ln2_b.reshape(depth, 1, d)
    b_router3 = b_router.reshape(depth, 1, n_exp)

    h = x.reshape(m, d)
    h_bf = h.astype(BF16)
    for l in range(depth):
        qkv = _matmul(h_bf, w_in, l, 0, 3 * bw, 512, 1024, BF16, "proj_qkv")
        rest = _matmul(h_bf, w_in, l, 3 * bw, 5 * bw, 512, 1024, F32, "proj_rest")
        o_a = _attention(qkv, batch, seq, heads, 256)
        o_bc = _gmlp_conv(rest, gm_g3, gm_b3, w_spatial, b_sp_bcast, conv_w, l, seq, bw, 256)
        merged = _merge(h_bf, o_a, o_bc, w_gate, b_gate4, w_branch, l, 512, 256)
        mix = _matmul(merged, w_out, l, 0, d, 512, 1024, F32, "out_proj")
        h, h_bf, h_pk = _residual_ln(h, mix, ln1_g3, ln1_b3, l, alpha, 256)

        idx, rank, wts, counts = _router(h, w_router, b_router3, l, 512)
        padded = (counts[0] + (tile - 1)) // tile * tile
        ends = jnp.cumsum(padded)
        offsets = ends - padded
        dest = (jnp.take(offsets, idx[:, :TOP_K]) + rank[:, :TOP_K]).reshape(-1)
        n_used = (ends[-1] // tile).astype(I32).reshape(1)
        tile_starts = jnp.minimum(jnp.arange(n_tiles, dtype=I32), n_used[0] - 1) * tile
        tile_expert = jnp.searchsorted(ends, tile_starts, side="right").astype(I32)
        src = _inverse_map(dest, n_rows)
        y_sorted = _experts(tile_expert, n_used, src, h_pk, w_e_gate, w_e_up, w_e_down, l, n_tiles, tile)
        act_s = _swiglu(h_bf, w_s_gate, w_s_up, l, 512, 512)
        shared = _matmul(act_s, w_s_down, l, 0, d, 512, 1024, F32, "shared_down")
        h, h_bf = _combine(dest, y_sorted, h, shared, wts, ln2_g3, ln2_b3, l, alpha, 128)
    return h.reshape(batch, seq, d)
```

```python
import functools

import jax
import jax.numpy as jnp
from jax import lax
from jax.experimental import pallas as pl
from jax.experimental.pallas import tpu as pltpu

F32 = jnp.float32
BF16 = jnp.bfloat16
I32 = jnp.int32

CHUNK = 64
HEAD_DIM = 128
GM_SPAN = 128
GM_GROUP_DIM = 128
IN_PARTS = 8
N_BRANCH = 3
TOP_K = 6
ROUTED_SCALE = 2.5
LN_EPS = 1e-5

V7X_VMEM_LIMIT_BYTES = 56 * 1024 * 1024
LANES = 128
SUBLANES = 8
CAST_ROWS = 256
EXPERT_ROW_TILE = 256
IDX_LANES = 8


def _params(sem):
    return pltpu.CompilerParams(dimension_semantics=sem, vmem_limit_bytes=V7X_VMEM_LIMIT_BYTES)


def _cast_rows_to_bf16(src_ref, dst_ref):
    k = src_ref.shape[0]
    rows = min(CAST_ROWS, k)

    def body(i, c):
        r = pl.multiple_of(i * rows, rows)
        dst_ref[pl.ds(r, rows), :] = src_ref[pl.ds(r, rows), :].astype(BF16)
        return c

    lax.fori_loop(0, k // rows, body, 0)


def _dot(a, b):
    return jnp.dot(a, b, preferred_element_type=F32)


PACK_COLS = 2 * LANES
HIGH_HALF = -65536


def _pack_bf16_pairs(lo, hi):
    lo_bits = lax.bitcast_convert_type(lo.astype(BF16).astype(F32), I32)
    hi_bits = lax.bitcast_convert_type(hi.astype(BF16).astype(F32), I32)
    return hi_bits | lax.shift_right_logical(lo_bits, 16)


def _unpack_bf16_pairs(word):
    lo = lax.bitcast_convert_type(word << 16, F32)
    hi = lax.bitcast_convert_type(word & HIGH_HALF, F32)
    return lo, hi


def _gather_pitch(packs):
    pitch = -(-packs // SUBLANES) * SUBLANES
    return pitch + SUBLANES if pitch % (2 * SUBLANES) == 0 else pitch


def _store_packed_rows(values, pk_ref):
    rows, cols = values.shape
    packs = cols // PACK_COLS
    for j in range(packs):
        c0 = j * PACK_COLS
        word = _pack_bf16_pairs(values[:, c0:c0 + LANES], values[:, c0 + LANES:c0 + PACK_COLS])
        pk_ref[pl.ds(j, rows, stride=packs), :] = word


def _mm_kernel(a_ref, w_ref, o_ref, wbf_ref):
    @pl.when(pl.program_id(1) == 0)
    def _():
        _cast_rows_to_bf16(w_ref, wbf_ref)

    o_ref[...] = _dot(a_ref[...], wbf_ref[...]).astype(o_ref.dtype)


def _matmul(a, w, layer, col_off, n_cols, tm, tn, out_dtype, name):
    m, k = a.shape
    tm = min(tm, m)
    while n_cols % tn or col_off % tn:
        tn //= 2
    col_block_off = col_off // tn
    return pl.pallas_call(
        _mm_kernel,
        grid=(n_cols // tn, m // tm),
        in_specs=[
            pl.BlockSpec((tm, k), lambda n, i: (i, 0)),
            pl.BlockSpec((None, k, tn), lambda n, i: (layer, 0, n + col_block_off),
                         pipeline_mode=pl.Buffered(1)),
        ],
        out_specs=pl.BlockSpec((tm, tn), lambda n, i: (i, n)),
        out_shape=jax.ShapeDtypeStruct((m, n_cols), out_dtype),
        scratch_shapes=[pltpu.VMEM((k, tn), BF16)],
        compiler_params=_params(("arbitrary", "arbitrary")),
        name=name,
    )(a, w)


def _attn_kernel(q_ref, k_ref, v_ref, o_ref, *, tq, kb, ks, scale):
    qi = pl.program_id(2)
    q = q_ref[...]
    n_sub = ks // kb
    n_steps = ((qi + 1) * tq + ks - 1) // ks
    t_pos = lax.broadcasted_iota(I32, (tq, kb), 0) + qi * tq
    s_loc = lax.broadcasted_iota(I32, (tq, kb), 1)
    r = lax.broadcasted_iota(I32, (2 * kb, 2 * kb), 0)
    c = lax.broadcasted_iota(I32, (2 * kb, 2 * kb), 1)
    r = jnp.where(r >= kb, r - kb, r)
    suffix = jnp.where((c >= kb) | (r > c), 1.0, 0.0).astype(BF16)

    def step(j, carry, masked):
        acc, later_sum = carry
        k0 = pl.multiple_of(j * ks, ks)
        kj = k_ref[pl.ds(k0, ks), :]
        vj = v_ref[pl.ds(k0, ks), :]
        z = lax.dot_general(q, kj, (((1,), (1,)), ((), ())), preferred_element_type=F32) * scale
        weights = [None] * n_sub
        for s in reversed(range(n_sub)):
            zs = z[:, s * kb:(s + 1) * kb]
            log_beta = jnp.minimum(zs, 0.0) - jnp.log1p(jnp.exp(-jnp.abs(zs)))
            log_one_minus = log_beta - zs
            if masked:
                strict = (s_loc + (k0 + s * kb)) < t_pos
                log_one_minus = jnp.where(strict, log_one_minus, 0.0)
            hi = log_one_minus.astype(BF16)
            lo = (log_one_minus - hi.astype(F32)).astype(BF16)
            sums = _dot(jnp.concatenate([hi, lo], axis=1), suffix)
            a = jnp.exp(log_beta + (sums[:, :kb] + later_sum))
            if masked:
                a = jnp.where(strict, a, 0.0)
            weights[s] = a.astype(BF16)
            later_sum = later_sum + sums[:, kb:]
        acc = acc + _dot(jnp.concatenate(weights, axis=1), vj)
        return acc, later_sum

    carry = (jnp.zeros((tq, HEAD_DIM), F32), jnp.zeros((tq, kb), F32))
    carry = step(n_steps - 1, carry, True)
    acc, _ = lax.fori_loop(0, n_steps - 1, lambda t, cr: step(n_steps - 2 - t, cr, False), carry)
    o_ref[...] = acc.astype(o_ref.dtype)


def _attention(qkv, batch, seq, heads, tq, ks):
    m = qkv.shape[0]
    tq = min(tq, seq)
    ks = min(ks, seq)
    assert ks % tq == 0
    kb = HEAD_DIM
    nq = seq // tq
    kern = functools.partial(_attn_kernel, tq=tq, kb=kb, ks=ks, scale=HEAD_DIM ** -0.5)
    return pl.pallas_call(
        kern,
        grid=(batch, heads, nq),
        in_specs=[
            pl.BlockSpec((tq, HEAD_DIM), lambda b, h, i: (b * nq + i, h)),
            pl.BlockSpec((seq, HEAD_DIM), lambda b, h, i: (b, heads + h)),
            pl.BlockSpec((seq, HEAD_DIM), lambda b, h, i: (b, 2 * heads + h)),
        ],
        out_specs=pl.BlockSpec((tq, HEAD_DIM), lambda b, h, i: (b * nq + i, h)),
        out_shape=jax.ShapeDtypeStruct((m, heads * HEAD_DIM), BF16),
        compiler_params=_params(("arbitrary", "arbitrary", "arbitrary")),
        name="sb_attention",
    )(qkv, qkv, qkv)


def _gelu_tanh(x):
    return 0.5 * x * (1.0 + jnp.tanh(0.7978845608028654 * (x + 0.044715 * (x * x * x))))


def _gmlp_conv_kernel(gu_ref, gv_ref, cb_ref, cc_ref, cx_ref, ccp_ref, cxp_ref, lng_ref, lnb_ref,
                      wsp_ref, bsp_ref, cw_ref, o_ref, *, tt, seq, groups):
    i = pl.program_id(0)
    z = cc_ref[...] * cx_ref[...]
    first = (i * tt) % seq == 0
    zp = jnp.where(first, 0.0, ccp_ref[...] * cxp_ref[...])
    row = lax.broadcasted_iota(I32, z.shape, 0)
    z1 = jnp.where(row == 0, zp[SUBLANES - 1:SUBLANES, :], pltpu.roll(z, 1, 0))
    z2 = pltpu.roll(z, 2, 0)
    z2 = jnp.where(row == 0, zp[SUBLANES - 2:SUBLANES - 1, :], z2)
    z2 = jnp.where(row == 1, zp[SUBLANES - 1:SUBLANES, :], z2)
    y = cw_ref[0:1, :] * z2 + cw_ref[1:2, :] * z1 + cw_ref[2:3, :] * z
    o_ref[1] = (cb_ref[...] * y).astype(o_ref.dtype)

    v = _gelu_tanh(gv_ref[...])
    mu = jnp.mean(v, axis=-1, keepdims=True)
    var = jnp.mean(jnp.square(v - mu), axis=-1, keepdims=True)
    vn = ((v - mu) * lax.rsqrt(var + LN_EPS) * lng_ref[...] + lnb_ref[...]).astype(BF16)
    u = _gelu_tanh(gu_ref[...])
    tr = lax.broadcasted_iota(I32, (GM_SPAN, GM_SPAN), 0) // CHUNK
    sc = lax.broadcasted_iota(I32, (GM_SPAN, GM_SPAN), 1) // CHUNK
    causal = tr >= sc
    for g in range(groups):
        wm = jnp.where(causal, wsp_ref[g], 0.0).astype(BF16)
        cols = slice(g * GM_GROUP_DIM, (g + 1) * GM_GROUP_DIM)
        for s in range(tt // GM_SPAN):
            rows = slice(s * GM_SPAN, (s + 1) * GM_SPAN)
            f = _dot(wm, vn[rows, cols]) + bsp_ref[g]
            o_ref[0, rows, cols] = (u[rows, cols] * f).astype(o_ref.dtype)


def _gmlp_conv(rest, gm_ln_g, gm_ln_b, w_spatial, b_spatial_bcast, conv_w, layer, seq, bw, tt):
    m = rest.shape[0]
    tt = min(tt, seq)
    groups = bw // GM_GROUP_DIM
    halo_blocks = tt // SUBLANES
    kern = functools.partial(_gmlp_conv_kernel, tt=tt, seq=seq, groups=groups)

    def part(p):
        return pl.BlockSpec((tt, bw), lambda i: (i, p))

    def halo(p):
        return pl.BlockSpec((SUBLANES, bw), lambda i: (jnp.maximum(i * halo_blocks - 1, 0), p))

    return pl.pallas_call(
        kern,
        grid=(m // tt,),
        in_specs=[
            part(0), part(1), part(2), part(3), part(4), halo(3), halo(4),
            pl.BlockSpec((None, 1, bw), lambda i: (layer, 0, 0)),
            pl.BlockSpec((None, 1, bw), lambda i: (layer, 0, 0)),
            pl.BlockSpec((None, groups, GM_SPAN, GM_SPAN), lambda i: (layer, 0, 0, 0)),
            pl.BlockSpec((None, groups, GM_SPAN, GM_SPAN), lambda i: (layer, 0, 0, 0)),
            pl.BlockSpec((None, 3, bw), lambda i: (layer, 0, 0)),
        ],
        out_specs=pl.BlockSpec((2, tt, bw), lambda i: (0, i, 0)),
        out_shape=jax.ShapeDtypeStruct((2, m, bw), BF16),
        compiler_params=_params(("arbitrary",)),
        name="gmlp_conv",
    )(rest, rest, rest, rest, rest, rest, rest, gm_ln_g, gm_ln_b, w_spatial, b_spatial_bcast, conv_w)


def _merge_kernel(h_ref, oa_ref, obc_ref, wg_ref, bg_ref, wb_ref, o_ref, wg_bf, wb_bf):
    @pl.when(pl.program_id(1) == 0)
    def _():
        for i in range(N_BRANCH):
            _cast_rows_to_bf16(wg_ref.at[i], wg_bf.at[i])
            _cast_rows_to_bf16(wb_ref.at[i], wb_bf.at[i])

    h = h_ref[...]
    merged = None
    for i in range(N_BRANCH):
        gate = jax.nn.sigmoid(_dot(h, wg_bf[i]) + bg_ref[i])
        branch = oa_ref[...] if i == 0 else obc_ref[i - 1]
        term = gate * _dot(branch, wb_bf[i])
        merged = term if merged is None else merged + term
    o_ref[...] = merged.astype(o_ref.dtype)


def _merge(h_bf, o_a, o_bc, w_gate, b_gate4, w_branch, layer, tm, tn):
    m, d = h_bf.shape
    bw = o_a.shape[1]
    tm = min(tm, m)
    tn = min(tn, d)
    return pl.pallas_call(
        _merge_kernel,
        grid=(d // tn, m // tm),
        in_specs=[
            pl.BlockSpec((tm, d), lambda n, i: (i, 0)),
            pl.BlockSpec((tm, bw), lambda n, i: (i, 0)),
            pl.BlockSpec((2, tm, bw), lambda n, i: (0, i, 0)),
            pl.BlockSpec((None, N_BRANCH, d, tn), lambda n, i: (layer, 0, 0, n),
                         pipeline_mode=pl.Buffered(1)),
            pl.BlockSpec((None, N_BRANCH, 1, tn), lambda n, i: (layer, 0, 0, n)),
            pl.BlockSpec((None, N_BRANCH, bw, tn), lambda n, i: (layer, 0, 0, n),
                         pipeline_mode=pl.Buffered(1)),
        ],
        out_specs=pl.BlockSpec((tm, tn), lambda n, i: (i, n)),
        out_shape=jax.ShapeDtypeStruct((m, d), BF16),
        scratch_shapes=[pltpu.VMEM((N_BRANCH, d, tn), BF16), pltpu.VMEM((N_BRANCH, bw, tn), BF16)],
        compiler_params=_params(("arbitrary", "arbitrary")),
        name="branch_merge",
    )(h_bf, o_a, o_bc, w_gate, b_gate4, w_branch)


def _layer_norm_rows(y, g, b):
    mu = jnp.mean(y, axis=-1, keepdims=True)
    var = jnp.mean(jnp.square(y - mu), axis=-1, keepdims=True)
    return (y - mu) * lax.rsqrt(var + LN_EPS) * g + b


def _residual_ln_kernel(h_ref, f_ref, g_ref, b_ref, o_ref, obf_ref, opk_ref, *, alpha):
    out = _layer_norm_rows(alpha * h_ref[...] + f_ref[...], g_ref[...], b_ref[...])
    o_ref[...] = out
    obf_ref[...] = out.astype(BF16)
    _store_packed_rows(out, opk_ref)


def _residual_ln(h, f, g, b, layer, alpha, tm):
    m, d = h.shape
    tm = min(tm, m)
    row = pl.BlockSpec((tm, d), lambda i: (i, 0))
    vec = pl.BlockSpec((None, 1, d), lambda i: (layer, 0, 0))
    packs = d // PACK_COLS
    packed = pl.BlockSpec((tm * packs, LANES), lambda i: (i, 0))
    return pl.pallas_call(
        functools.partial(_residual_ln_kernel, alpha=alpha),
        grid=(m // tm,),
        in_specs=[row, row, vec, vec],
        out_specs=[row, row, packed],
        out_shape=[jax.ShapeDtypeStruct((m, d), F32), jax.ShapeDtypeStruct((m, d), BF16),
                   jax.ShapeDtypeStruct((m * packs, LANES), I32)],
        compiler_params=_params(("arbitrary",)),
        name="residual_ln",
    )(h, f, g, b)


def _router_kernel(h_ref, w_ref, b_ref, idx_ref, rank_ref, wts_ref, cnt_ref, run_ref, *, tr, n_exp):
    @pl.when(pl.program_id(0) == 0)
    def _():
        run_ref[...] = jnp.zeros_like(run_ref)

    h = h_ref[...]
    w = w_ref[...]
    h_hi = h.astype(BF16)
    h_lo = (h - h_hi.astype(F32)).astype(BF16)
    w_hi = w.astype(BF16)
    w_lo = (w - w_hi.astype(F32)).astype(BF16)
    logits = _dot(h_hi, w_hi) + (_dot(h_hi, w_lo) + _dot(h_lo, w_hi))
    scores = jax.nn.sigmoid(logits)
    sel = scores + b_ref[...]
    lane = lax.broadcasted_iota(I32, (tr, n_exp), 1).astype(F32)
    out_lane = lax.broadcasted_iota(I32, (tr, IDX_LANES), 1)
    mask = jnp.zeros((tr, n_exp), F32)
    idx_out = jnp.zeros((tr, IDX_LANES), F32)
    picks = []
    for k in range(TOP_K):
        best = jnp.max(sel, axis=-1, keepdims=True)
        pick = jnp.min(jnp.where(sel == best, lane, float(n_exp)), axis=-1, keepdims=True)
        hit = lane == pick
        mask = jnp.where(hit, 1.0, mask)
        sel = jnp.where(hit, -jnp.inf, sel)
        idx_out = jnp.where(out_lane == k, pick, idx_out)
        picks.append(hit)
    chosen_scores = mask * scores
    gates = chosen_scores / jnp.sum(chosen_scores, axis=-1, keepdims=True) * ROUTED_SCALE

    rr = lax.broadcasted_iota(I32, (tr, tr), 0)
    cc = lax.broadcasted_iota(I32, (tr, tr), 1)
    lower = jnp.where(rr > cc, 1.0, 0.0).astype(BF16)
    rank = _dot(lower, mask.astype(BF16)) + run_ref[...]
    run_ref[...] = run_ref[...] + jnp.sum(mask, axis=0, keepdims=True)
    cnt_ref[...] = run_ref[...].astype(I32)

    rank_out = jnp.zeros((tr, IDX_LANES), F32)
    wts_out = jnp.zeros((tr, IDX_LANES), F32)
    for k in range(TOP_K):
        rank_out = jnp.where(out_lane == k, jnp.sum(jnp.where(picks[k], rank, 0.0), axis=-1, keepdims=True), rank_out)
        wts_out = jnp.where(out_lane == k, jnp.sum(jnp.where(picks[k], gates, 0.0), axis=-1, keepdims=True), wts_out)
    idx_ref[...] = idx_out.astype(I32)
    rank_ref[...] = rank_out.astype(I32)
    wts_ref[...] = wts_out


def _router(h, w_router, b_router3, layer, tr):
    m, d = h.shape
    n_exp = w_router.shape[-1]
    tr = min(tr, m)
    lists = pl.BlockSpec((tr, IDX_LANES), lambda i: (i, 0))
    return pl.pallas_call(
        functools.partial(_router_kernel, tr=tr, n_exp=n_exp),
        grid=(m // tr,),
        in_specs=[
            pl.BlockSpec((tr, d), lambda i: (i, 0)),
            pl.BlockSpec((None, d, n_exp), lambda i: (layer, 0, 0)),
            pl.BlockSpec((None, 1, n_exp), lambda i: (layer, 0, 0)),
        ],
        out_specs=[lists, lists, lists, pl.BlockSpec((1, n_exp), lambda i: (0, 0))],
        out_shape=[
            jax.ShapeDtypeStruct((m, IDX_LANES), I32),
            jax.ShapeDtypeStruct((m, IDX_LANES), I32),
            jax.ShapeDtypeStruct((m, IDX_LANES), F32),
            jax.ShapeDtypeStruct((1, n_exp), I32),
        ],
        scratch_shapes=[pltpu.VMEM((1, n_exp), F32)],
        compiler_params=_params(("arbitrary",)),
        name="router_topk",
    )(h, w_router, b_router3)


def _inverse_kernel(dest_ref, zeros_hbm, src_ref, sem, *, n_pairs):
    clear = pltpu.make_async_copy(zeros_hbm, src_ref, sem)
    clear.start()
    clear.wait()

    def place(t, c):
        for k in range(TOP_K):
            src_ref[dest_ref[t * TOP_K + k]] = t
        return c

    lax.fori_loop(0, n_pairs // TOP_K, place, 0, unroll=2)


def _inverse_map(dest_flat, n_rows):
    n_pairs = dest_flat.shape[0]
    smem = pl.BlockSpec(memory_space=pltpu.SMEM)
    return pl.pallas_call(
        functools.partial(_inverse_kernel, n_pairs=n_pairs),
        in_specs=[smem, pl.BlockSpec(memory_space=pl.ANY)],
        out_specs=smem,
        out_shape=jax.ShapeDtypeStruct((n_rows,), I32),
        scratch_shapes=[pltpu.SemaphoreType.DMA(())],
        name="inverse_map",
    )(dest_flat, jnp.zeros((n_rows,), I32))


def _route_plan_kernel(cnt_ref, idx_ref, rank_ref, dest_ref, te_ref, nu_ref, off_ref, *,
                       n_exp, tile, n_tiles):
    shift = tile.bit_length() - 1

    def per_expert(e, carry):
        tiles_before, last_owner = carry
        nt = lax.shift_right_logical(cnt_ref[e] + (tile - 1), shift)
        off_ref[e] = tiles_before * tile

        def fill(q, c):
            te_ref[tiles_before + q] = e
            return c

        lax.fori_loop(0, nt, fill, 0)
        return tiles_before + nt, jnp.where(nt > 0, e, last_owner)

    used, last_owner = lax.fori_loop(0, n_exp, per_expert, (jnp.int32(0), jnp.int32(0)))
    nu_ref[0] = used

    def fill_tail(q, c):
        te_ref[q] = last_owner
        return c

    lax.fori_loop(used, n_tiles, fill_tail, 0)

    idx = idx_ref[...]
    offset = lax.fori_loop(0, n_exp, lambda e, acc: jnp.where(idx == e, off_ref[e], acc),
                           jnp.zeros_like(idx))
    dest_ref[...] = offset + rank_ref[...]


def _route_plan(counts, idx, rank, tile, n_tiles):
    n_exp = counts.shape[-1]
    pairs = idx[:, :TOP_K].reshape(-1, LANES)
    ranks = rank[:, :TOP_K].reshape(-1, LANES)
    smem = pl.BlockSpec(memory_space=pltpu.SMEM)
    vmem = pl.BlockSpec(memory_space=pltpu.VMEM)
    assert tile & (tile - 1) == 0
    return pl.pallas_call(
        functools.partial(_route_plan_kernel, n_exp=n_exp, tile=tile, n_tiles=n_tiles),
        in_specs=[smem, vmem, vmem],
        out_specs=[vmem, smem, smem],
        out_shape=[jax.ShapeDtypeStruct(pairs.shape, I32), jax.ShapeDtypeStruct((n_tiles,), I32),
                   jax.ShapeDtypeStruct((1,), I32)],
        scratch_shapes=[pltpu.SMEM((n_exp,), I32)],
        name="route_plan",
    )(counts.reshape(n_exp), pairs, ranks)


def _expert_kernel(te_ref, nu_ref, src_ref, h_hbm, wg_ref, wu_ref, wd_ref, o_ref,
                   xbuf, x_bf, wg_bf, wu_bf, wd_bf, sem, *, tile):
    i = pl.program_id(0)
    n_used = nu_ref[0]
    slot = i % 2

    packs = x_bf.shape[1] // PACK_COLS
    pitch = xbuf.shape[0] // (2 * tile)

    def start_gather(tile_idx, s):
        def body(r, c):
            src_row = pl.multiple_of(src_ref[tile_idx * tile + r] * packs, packs)
            dst_row = pl.multiple_of((s * tile + r) * pitch, SUBLANES)
            pltpu.make_async_copy(h_hbm.at[pl.ds(src_row, packs)], xbuf.at[pl.ds(dst_row, packs)],
                                  sem.at[s]).start()
            return c
        lax.fori_loop(0, tile, body, 0, unroll=8)

    def wait_gather(s):
        dst_row = pl.multiple_of(s * (tile * pitch), SUBLANES)
        pltpu.make_async_copy(h_hbm.at[pl.ds(0, tile * packs)], xbuf.at[pl.ds(dst_row, tile * packs)],
                              sem.at[s]).wait()

    @pl.when(i == 0)
    def _():
        start_gather(0, 0)

    @pl.when(i + 1 < n_used)
    def _():
        start_gather(i + 1, 1 - slot)

    @pl.when(i < n_used)
    def _():
        changed = jnp.logical_or(i == 0, te_ref[i] != te_ref[jnp.maximum(i - 1, 0)])

        @pl.when(changed)
        def _():
            _cast_rows_to_bf16(wg_ref, wg_bf)
            _cast_rows_to_bf16(wu_ref, wu_bf)
            _cast_rows_to_bf16(wd_ref, wd_bf)

        wait_gather(slot)
        for j in range(packs):
            lo, hi = _unpack_bf16_pairs(xbuf[pl.ds(slot * (tile * pitch) + j, tile, stride=pitch), :])
            c0 = j * PACK_COLS
            x_bf[:, c0:c0 + LANES] = lo.astype(BF16)
            x_bf[:, c0 + LANES:c0 + PACK_COLS] = hi.astype(BF16)
        x = x_bf[...]
        act = jax.nn.silu(_dot(x, wg_bf[...])) * _dot(x, wu_bf[...])
        _store_packed_rows(_dot(act.astype(BF16), wd_bf[...]), o_ref)

    @pl.when(i >= n_used)
    def _():
        o_ref[...] = jnp.zeros_like(o_ref)


def _experts(tile_expert, n_used, src, h_pk, w_e_gate, w_e_up, w_e_down, layer, n_tiles, tile):
    d, ff = w_e_gate.shape[-2:]
    packs = d // PACK_COLS

    grid_spec = pltpu.PrefetchScalarGridSpec(
        num_scalar_prefetch=3,
        grid=(n_tiles,),
        in_specs=[
            pl.BlockSpec(memory_space=pl.ANY),
            pl.BlockSpec((None, None, d, ff), lambda i, te, nu, s: (layer, te[i], 0, 0)),
            pl.BlockSpec((None, None, d, ff), lambda i, te, nu, s: (layer, te[i], 0, 0)),
            pl.BlockSpec((None, None, ff, d), lambda i, te, nu, s: (layer, te[i], 0, 0)),
        ],
        out_specs=pl.BlockSpec((tile * packs, LANES), lambda i, te, nu, s: (i, 0)),
        scratch_shapes=[
            pltpu.VMEM((2 * tile * _gather_pitch(packs), LANES), I32),
            pltpu.VMEM((tile, d), BF16),
            pltpu.VMEM((d, ff), BF16),
            pltpu.VMEM((d, ff), BF16),
            pltpu.VMEM((ff, d), BF16),
            pltpu.SemaphoreType.DMA((2,)),
        ],
    )
    return pl.pallas_call(
        functools.partial(_expert_kernel, tile=tile),
        grid_spec=grid_spec,
        out_shape=jax.ShapeDtypeStruct((n_tiles * tile * packs, LANES), I32),
        compiler_params=_params(("arbitrary",)),
        name="expert_mlp",
    )(tile_expert, n_used, src, h_pk, w_e_gate, w_e_up, w_e_down)


def _swiglu_kernel(a_ref, wg_ref, wu_ref, o_ref, wg_bf, wu_bf):
    @pl.when(pl.program_id(1) == 0)
    def _():
        _cast_rows_to_bf16(wg_ref, wg_bf)
        _cast_rows_to_bf16(wu_ref, wu_bf)

    a = a_ref[...]
    o_ref[...] = (jax.nn.silu(_dot(a, wg_bf[...])) * _dot(a, wu_bf[...])).astype(o_ref.dtype)


def _swiglu(a, w_gate, w_up, layer, tm, tn):
    m, k = a.shape
    n = w_gate.shape[-1]
    tm = min(tm, m)
    tn = min(tn, n)
    wspec = pl.BlockSpec((None, k, tn), lambda j, i: (layer, 0, j), pipeline_mode=pl.Buffered(1))
    return pl.pallas_call(
        _swiglu_kernel,
        grid=(n // tn, m // tm),
        in_specs=[pl.BlockSpec((tm, k), lambda j, i: (i, 0)), wspec, wspec],
        out_specs=pl.BlockSpec((tm, tn), lambda j, i: (i, j)),
        out_shape=jax.ShapeDtypeStruct((m, n), BF16),
        scratch_shapes=[pltpu.VMEM((k, tn), BF16), pltpu.VMEM((k, tn), BF16)],
        compiler_params=_params(("arbitrary", "arbitrary")),
        name="shared_swiglu",
    )(a, w_gate, w_up)


def _combine_kernel(dest_ref, y_hbm, h_ref, sh_ref, wts_ref, g_ref, b_ref, o_ref, obf_ref,
                    ybuf, routed, sem, *, tc, alpha):
    i = pl.program_id(0)
    n_steps = pl.num_programs(0)
    slot = i % 2

    packs = routed.shape[1] // PACK_COLS
    pitch = ybuf.shape[0] // (2 * TOP_K * tc)

    def start_gather(step, s):
        def body(t, c):
            for k in range(TOP_K):
                src_row = pl.multiple_of(dest_ref[(step * tc + t) * TOP_K + k] * packs, packs)
                dst_row = pl.multiple_of(((s * TOP_K + k) * tc + t) * pitch, SUBLANES)
                pltpu.make_async_copy(y_hbm.at[pl.ds(src_row, packs)], ybuf.at[pl.ds(dst_row, packs)],
                                      sem.at[s]).start()
            return c
        lax.fori_loop(0, tc, body, 0, unroll=2)

    def wait_gather(s):
        dst_row = pl.multiple_of(s * (TOP_K * tc * pitch), SUBLANES)
        pltpu.make_async_copy(y_hbm.at[pl.ds(0, TOP_K * tc * packs)],
                              ybuf.at[pl.ds(dst_row, TOP_K * tc * packs)], sem.at[s]).wait()

    @pl.when(i == 0)
    def _():
        start_gather(0, 0)

    @pl.when(i + 1 < n_steps)
    def _():
        start_gather(i + 1, 1 - slot)

    wait_gather(slot)
    wts = wts_ref[...]
    for j in range(packs):
        acc_lo = None
        for k in range(TOP_K):
            base = (slot * TOP_K + k) * (tc * pitch) + j
            lo, hi = _unpack_bf16_pairs(ybuf[pl.ds(base, tc, stride=pitch), :])
            w = wts[:, k:k + 1]
            acc_lo = w * lo if acc_lo is None else acc_lo + w * lo
            acc_hi = w * hi if k == 0 else acc_hi + w * hi
        c0 = j * PACK_COLS
        routed[:, c0:c0 + LANES] = acc_lo
        routed[:, c0 + LANES:c0 + PACK_COLS] = acc_hi
    y = alpha * h_ref[...] + (routed[...] + sh_ref[...])
    out = _layer_norm_rows(y, g_ref[...], b_ref[...])
    o_ref[...] = out
    obf_ref[...] = out.astype(BF16)


def _combine(dest_flat, y_sorted, h, shared, wts, g, b, layer, alpha, tc):
    m, d = h.shape
    tc = min(tc, m)
    row = pl.BlockSpec((tc, d), lambda i, dest: (i, 0))
    vec = pl.BlockSpec((None, 1, d), lambda i, dest: (layer, 0, 0))
    grid_spec = pltpu.PrefetchScalarGridSpec(
        num_scalar_prefetch=1,
        grid=(m // tc,),
        in_specs=[
            pl.BlockSpec(memory_space=pl.ANY),
            row, row,
            pl.BlockSpec((tc, IDX_LANES), lambda i, dest: (i, 0)),
            vec, vec,
        ],
        out_specs=[row, row],
        scratch_shapes=[pltpu.VMEM((2 * TOP_K * tc * _gather_pitch(d // PACK_COLS), LANES), I32),
                        pltpu.VMEM((tc, d), F32), pltpu.SemaphoreType.DMA((2,))],
    )
    return pl.pallas_call(
        functools.partial(_combine_kernel, tc=tc, alpha=alpha),
        grid_spec=grid_spec,
        out_shape=[jax.ShapeDtypeStruct((m, d), F32), jax.ShapeDtypeStruct((m, d), BF16)],
        compiler_params=_params(("arbitrary",)),
        name="moe_combine_ln",
    )(dest_flat, y_sorted, h, shared, wts, g, b)


def kernel(x, w_in, w_gate, b_gate, w_branch, w_out, gm_ln_g, gm_ln_b, w_spatial, b_spatial, conv_w,
           ln1_g, ln1_b, w_router, b_router, w_e_gate, w_e_up, w_e_down, w_s_gate, w_s_up, w_s_down,
           ln2_g, ln2_b):
    batch, seq, d = x.shape
    depth = w_in.shape[0]
    m = batch * seq
    bw = d // 4
    heads = bw // HEAD_DIM
    n_exp = w_router.shape[-1]
    alpha = float((2 * depth) ** 0.25)
    tile = EXPERT_ROW_TILE
    n_tiles = (m * TOP_K + n_exp * (tile - 1)) // tile + 1
    n_rows = n_tiles * tile

    b_gate4 = b_gate.reshape(depth, N_BRANCH, 1, d)
    gm_g3 = gm_ln_g.reshape(depth, 1, bw)
    gm_b3 = gm_ln_b.reshape(depth, 1, bw)
    b_sp_bcast = jnp.broadcast_to(b_spatial[..., None], b_spatial.shape + (GM_SPAN,))
    ln1_g3, ln1_b3 = ln1_g.reshape(depth, 1, d), ln1_b.reshape(depth, 1, d)
    ln2_g3, ln2_b3 = ln2_g.reshape(depth, 1, d), ln2_b.reshape(depth, 1, d)
    b_router3 = b_router.reshape(depth, 1, n_exp)

    h = x.reshape(m, d)
    h_bf = h.astype(BF16)
    for l in range(depth):
        qkv = _matmul(h_bf, w_in, l, 0, 3 * bw, 512, 1024, BF16, "proj_qkv")
        rest = _matmul(h_bf, w_in, l, 3 * bw, 5 * bw, 512, 1024, F32, "proj_rest")
        o_a = _attention(qkv, batch, seq, heads, 256, 512)
        o_bc = _gmlp_conv(rest, gm_g3, gm_b3, w_spatial, b_sp_bcast, conv_w, l, seq, bw, 256)
        merged = _merge(h_bf, o_a, o_bc, w_gate, b_gate4, w_branch, l, 512, 256)
        mix = _matmul(merged, w_out, l, 0, d, 512, 1024, F32, "out_proj")
        h, h_bf, h_pk = _residual_ln(h, mix, ln1_g3, ln1_b3, l, alpha, 256)

        idx, rank, wts, counts = _router(h, w_router, b_router3, l, 512)
        dest2d, tile_expert, n_used = _route_plan(counts, idx, rank, tile, n_tiles)
        dest = dest2d.reshape(-1)
        src = _inverse_map(dest, n_rows)
        y_sorted = _experts(tile_expert, n_used, src, h_pk, w_e_gate, w_e_up, w_e_down, l, n_tiles, tile)
        act_s = _swiglu(h_bf, w_s_gate, w_s_up, l, 512, 512)
        shared = _matmul(act_s, w_s_down, l, 0, d, 512, 1024, F32, "shared_down")
        h, h_bf = _combine(dest, y_sorted, h, shared, wts, ln2_g3, ln2_b3, l, alpha, 128)
    return h.reshape(batch, seq, d)
```

```python
import functools

import jax
import jax.numpy as jnp
from jax import lax
from jax.experimental import pallas as pl
from jax.experimental.pallas import tpu as pltpu

F32 = jnp.float32
BF16 = jnp.bfloat16
I32 = jnp.int32

CHUNK = 64
HEAD_DIM = 128
GM_SPAN = 128
GM_GROUP_DIM = 128
IN_PARTS = 8
N_BRANCH = 3
TOP_K = 6
ROUTED_SCALE = 2.5
LN_EPS = 1e-5
LOG2_E = 1.4426950408889634

V7X_VMEM_LIMIT_BYTES = 56 * 1024 * 1024
LANES = 128
SUBLANES = 8
CAST_ROWS = 256
EXPERT_ROW_TILE = 256
IDX_LANES = 8


def _params(sem):
    return pltpu.CompilerParams(dimension_semantics=sem, vmem_limit_bytes=V7X_VMEM_LIMIT_BYTES)


def _cast_rows_to_bf16(src_ref, dst_ref):
    k = src_ref.shape[0]
    rows = min(CAST_ROWS, k)

    def body(i, c):
        r = pl.multiple_of(i * rows, rows)
        dst_ref[pl.ds(r, rows), :] = src_ref[pl.ds(r, rows), :].astype(BF16)
        return c

    lax.fori_loop(0, k // rows, body, 0)


def _dot(a, b):
    return jnp.dot(a, b, preferred_element_type=F32)


PACK_COLS = 2 * LANES
HIGH_HALF = -65536


def _pack_bf16_pairs(lo, hi):
    lo_bits = lax.bitcast_convert_type(lo.astype(BF16).astype(F32), I32)
    hi_bits = lax.bitcast_convert_type(hi.astype(BF16).astype(F32), I32)
    return hi_bits | lax.shift_right_logical(lo_bits, 16)


def _unpack_bf16_pairs(word):
    lo = lax.bitcast_convert_type(word << 16, F32)
    hi = lax.bitcast_convert_type(word & HIGH_HALF, F32)
    return lo, hi


def _gather_pitch(packs):
    pitch = -(-packs // SUBLANES) * SUBLANES
    return pitch + SUBLANES if pitch % (2 * SUBLANES) == 0 else pitch


def _store_packed_rows(values, pk_ref):
    rows, cols = values.shape
    packs = cols // PACK_COLS
    for j in range(packs):
        c0 = j * PACK_COLS
        word = _pack_bf16_pairs(values[:, c0:c0 + LANES], values[:, c0 + LANES:c0 + PACK_COLS])
        pk_ref[pl.ds(j, rows, stride=packs), :] = word


def _mm_kernel(a_ref, w_ref, o_ref, wbf_ref):
    @pl.when(pl.program_id(1) == 0)
    def _():
        _cast_rows_to_bf16(w_ref, wbf_ref)

    o_ref[...] = _dot(a_ref[...], wbf_ref[...]).astype(o_ref.dtype)


def _matmul(a, w, layer, col_off, n_cols, tm, tn, out_dtype, name):
    m, k = a.shape
    tm = min(tm, m)
    while n_cols % tn or col_off % tn:
        tn //= 2
    col_block_off = col_off // tn
    return pl.pallas_call(
        _mm_kernel,
        grid=(n_cols // tn, m // tm),
        in_specs=[
            pl.BlockSpec((tm, k), lambda n, i: (i, 0)),
            pl.BlockSpec((None, k, tn), lambda n, i: (layer, 0, n + col_block_off),
                         pipeline_mode=pl.Buffered(1)),
        ],
        out_specs=pl.BlockSpec((tm, tn), lambda n, i: (i, n)),
        out_shape=jax.ShapeDtypeStruct((m, n_cols), out_dtype),
        scratch_shapes=[pltpu.VMEM((k, tn), BF16)],
        compiler_params=_params(("arbitrary", "arbitrary")),
        name=name,
    )(a, w)


def _attn_kernel(q_ref, k_ref, v_ref, o_ref, *, tq, kb, ks, scale):
    qi = pl.program_id(2)
    q = q_ref[...]
    n_sub = ks // kb
    n_steps = ((qi + 1) * tq + ks - 1) // ks
    t_pos = lax.broadcasted_iota(I32, (tq, kb), 0) + qi * tq
    s_loc = lax.broadcasted_iota(I32, (tq, kb), 1)
    r = lax.broadcasted_iota(I32, (2 * kb, 2 * kb), 0)
    c = lax.broadcasted_iota(I32, (2 * kb, 2 * kb), 1)
    r = jnp.where(r >= kb, r - kb, r)
    suffix = jnp.where((c >= kb) | (r > c), 1.0, 0.0).astype(BF16)

    def step(j, carry, masked):
        acc, later_sum = carry
        k0 = pl.multiple_of(j * ks, ks)
        kj = k_ref[pl.ds(k0, ks), :]
        vj = v_ref[pl.ds(k0, ks), :]
        z = lax.dot_general(q, kj, (((1,), (1,)), ((), ())), preferred_element_type=F32) * (scale * LOG2_E)
        weights = [None] * n_sub
        for s in reversed(range(n_sub)):
            zs = z[:, s * kb:(s + 1) * kb]
            softplus = jnp.log(1.0 + jnp.exp2(jnp.minimum(zs, -zs))) * LOG2_E
            log_beta = jnp.minimum(zs, 0.0) - softplus
            log_one_minus = log_beta - zs
            if masked:
                strict = (s_loc + (k0 + s * kb)) < t_pos
                log_one_minus = jnp.where(strict, log_one_minus, 0.0)
            hi = log_one_minus.astype(BF16)
            lo = (log_one_minus - hi.astype(F32)).astype(BF16)
            sums = _dot(jnp.concatenate([hi, lo], axis=1), suffix)
            a = jnp.exp2(log_beta + (sums[:, :kb] + later_sum))
            if masked:
                a = jnp.where(strict, a, 0.0)
            weights[s] = a.astype(BF16)
            later_sum = later_sum + sums[:, kb:]
        acc = acc + _dot(jnp.concatenate(weights, axis=1), vj)
        return acc, later_sum

    carry = (jnp.zeros((tq, HEAD_DIM), F32), jnp.zeros((tq, kb), F32))
    carry = step(n_steps - 1, carry, True)
    acc, _ = lax.fori_loop(0, n_steps - 1, lambda t, cr: step(n_steps - 2 - t, cr, False), carry)
    o_ref[...] = acc.astype(o_ref.dtype)


def _attention(qkv, batch, seq, heads, tq, ks):
    m = qkv.shape[0]
    tq = min(tq, seq)
    ks = min(ks, seq)
    assert ks % tq == 0
    kb = HEAD_DIM
    nq = seq // tq
    kern = functools.partial(_attn_kernel, tq=tq, kb=kb, ks=ks, scale=HEAD_DIM ** -0.5)
    return pl.pallas_call(
        kern,
        grid=(batch, heads, nq),
        in_specs=[
            pl.BlockSpec((tq, HEAD_DIM), lambda b, h, i: (b * nq + i, h)),
            pl.BlockSpec((seq, HEAD_DIM), lambda b, h, i: (b, heads + h)),
            pl.BlockSpec((seq, HEAD_DIM), lambda b, h, i: (b, 2 * heads + h)),
        ],
        out_specs=pl.BlockSpec((tq, HEAD_DIM), lambda b, h, i: (b * nq + i, h)),
        out_shape=jax.ShapeDtypeStruct((m, heads * HEAD_DIM), BF16),
        compiler_params=_params(("arbitrary", "arbitrary", "arbitrary")),
        name="sb_attention",
    )(qkv, qkv, qkv)


def _gelu_tanh(x):
    return 0.5 * x * (1.0 + jnp.tanh(0.7978845608028654 * (x + 0.044715 * (x * x * x))))


def _gmlp_conv_kernel(gu_ref, gv_ref, cb_ref, cc_ref, cx_ref, ccp_ref, cxp_ref, lng_ref, lnb_ref,
                      wsp_ref, bsp_ref, cw_ref, o_ref, *, tt, seq, groups):
    i = pl.program_id(0)
    z = cc_ref[...] * cx_ref[...]
    first = (i * tt) % seq == 0
    zp = jnp.where(first, 0.0, ccp_ref[...] * cxp_ref[...])
    row = lax.broadcasted_iota(I32, z.shape, 0)
    z1 = jnp.where(row == 0, zp[SUBLANES - 1:SUBLANES, :], pltpu.roll(z, 1, 0))
    z2 = pltpu.roll(z, 2, 0)
    z2 = jnp.where(row == 0, zp[SUBLANES - 2:SUBLANES - 1, :], z2)
    z2 = jnp.where(row == 1, zp[SUBLANES - 1:SUBLANES, :], z2)
    y = cw_ref[0:1, :] * z2 + cw_ref[1:2, :] * z1 + cw_ref[2:3, :] * z
    o_ref[1] = (cb_ref[...] * y).astype(o_ref.dtype)

    v = _gelu_tanh(gv_ref[...])
    mu = jnp.mean(v, axis=-1, keepdims=True)
    var = jnp.mean(jnp.square(v - mu), axis=-1, keepdims=True)
    vn = ((v - mu) * lax.rsqrt(var + LN_EPS) * lng_ref[...] + lnb_ref[...]).astype(BF16)
    u = _gelu_tanh(gu_ref[...])
    tr = lax.broadcasted_iota(I32, (GM_SPAN, GM_SPAN), 0) // CHUNK
    sc = lax.broadcasted_iota(I32, (GM_SPAN, GM_SPAN), 1) // CHUNK
    causal = tr >= sc
    for g in range(groups):
        wm = jnp.where(causal, wsp_ref[g], 0.0).astype(BF16)
        cols = slice(g * GM_GROUP_DIM, (g + 1) * GM_GROUP_DIM)
        for s in range(tt // GM_SPAN):
            rows = slice(s * GM_SPAN, (s + 1) * GM_SPAN)
            f = _dot(wm, vn[rows, cols]) + bsp_ref[g]
            o_ref[0, rows, cols] = (u[rows, cols] * f).astype(o_ref.dtype)


def _gmlp_conv(rest, gm_ln_g, gm_ln_b, w_spatial, b_spatial_bcast, conv_w, layer, seq, bw, tt):
    m = rest.shape[0]
    tt = min(tt, seq)
    groups = bw // GM_GROUP_DIM
    halo_blocks = tt // SUBLANES
    kern = functools.partial(_gmlp_conv_kernel, tt=tt, seq=seq, groups=groups)

    def part(p):
        return pl.BlockSpec((tt, bw), lambda i: (i, p))

    def halo(p):
        return pl.BlockSpec((SUBLANES, bw), lambda i: (jnp.maximum(i * halo_blocks - 1, 0), p))

    return pl.pallas_call(
        kern,
        grid=(m // tt,),
        in_specs=[
            part(0), part(1), part(2), part(3), part(4), halo(3), halo(4),
            pl.BlockSpec((None, 1, bw), lambda i: (layer, 0, 0)),
            pl.BlockSpec((None, 1, bw), lambda i: (layer, 0, 0)),
            pl.BlockSpec((None, groups, GM_SPAN, GM_SPAN), lambda i: (layer, 0, 0, 0)),
            pl.BlockSpec((None, groups, GM_SPAN, GM_SPAN), lambda i: (layer, 0, 0, 0)),
            pl.BlockSpec((None, 3, bw), lambda i: (layer, 0, 0)),
        ],
        out_specs=pl.BlockSpec((2, tt, bw), lambda i: (0, i, 0)),
        out_shape=jax.ShapeDtypeStruct((2, m, bw), BF16),
        compiler_params=_params(("arbitrary",)),
        name="gmlp_conv",
    )(rest, rest, rest, rest, rest, rest, rest, gm_ln_g, gm_ln_b, w_spatial, b_spatial_bcast, conv_w)


def _merge_kernel(h_ref, oa_ref, obc_ref, wg_ref, bg_ref, wb_ref, o_ref, wg_bf, wb_bf):
    @pl.when(pl.program_id(1) == 0)
    def _():
        for i in range(N_BRANCH):
            _cast_rows_to_bf16(wg_ref.at[i], wg_bf.at[i])
            _cast_rows_to_bf16(wb_ref.at[i], wb_bf.at[i])

    h = h_ref[...]
    merged = None
    for i in range(N_BRANCH):
        gate = jax.nn.sigmoid(_dot(h, wg_bf[i]) + bg_ref[i])
        branch = oa_ref[...] if i == 0 else obc_ref[i - 1]
        term = gate * _dot(branch, wb_bf[i])
        merged = term if merged is None else merged + term
    o_ref[...] = merged.astype(o_ref.dtype)


def _merge(h_bf, o_a, o_bc, w_gate, b_gate4, w_branch, layer, tm, tn):
    m, d = h_bf.shape
    bw = o_a.shape[1]
    tm = min(tm, m)
    tn = min(tn, d)
    return pl.pallas_call(
        _merge_kernel,
        grid=(d // tn, m // tm),
        in_specs=[
            pl.BlockSpec((tm, d), lambda n, i: (i, 0)),
            pl.BlockSpec((tm, bw), lambda n, i: (i, 0)),
            pl.BlockSpec((2, tm, bw), lambda n, i: (0, i, 0)),
            pl.BlockSpec((None, N_BRANCH, d, tn), lambda n, i: (layer, 0, 0, n),
                         pipeline_mode=pl.Buffered(1)),
            pl.BlockSpec((None, N_BRANCH, 1, tn), lambda n, i: (layer, 0, 0, n)),
            pl.BlockSpec((None, N_BRANCH, bw, tn), lambda n, i: (layer, 0, 0, n),
                         pipeline_mode=pl.Buffered(1)),
        ],
        out_specs=pl.BlockSpec((tm, tn), lambda n, i: (i, n)),
        out_shape=jax.ShapeDtypeStruct((m, d), BF16),
        scratch_shapes=[pltpu.VMEM((N_BRANCH, d, tn), BF16), pltpu.VMEM((N_BRANCH, bw, tn), BF16)],
        compiler_params=_params(("arbitrary", "arbitrary")),
        name="branch_merge",
    )(h_bf, o_a, o_bc, w_gate, b_gate4, w_branch)


def _layer_norm_rows(y, g, b):
    mu = jnp.mean(y, axis=-1, keepdims=True)
    var = jnp.mean(jnp.square(y - mu), axis=-1, keepdims=True)
    return (y - mu) * lax.rsqrt(var + LN_EPS) * g + b


def _residual_ln_kernel(h_ref, f_ref, g_ref, b_ref, o_ref, obf_ref, opk_ref, *, alpha):
    out = _layer_norm_rows(alpha * h_ref[...] + f_ref[...], g_ref[...], b_ref[...])
    o_ref[...] = out
    obf_ref[...] = out.astype(BF16)
    _store_packed_rows(out, opk_ref)


def _residual_ln(h, f, g, b, layer, alpha, tm):
    m, d = h.shape
    tm = min(tm, m)
    row = pl.BlockSpec((tm, d), lambda i: (i, 0))
    vec = pl.BlockSpec((None, 1, d), lambda i: (layer, 0, 0))
    packs = d // PACK_COLS
    packed = pl.BlockSpec((tm * packs, LANES), lambda i: (i, 0))
    return pl.pallas_call(
        functools.partial(_residual_ln_kernel, alpha=alpha),
        grid=(m // tm,),
        in_specs=[row, row, vec, vec],
        out_specs=[row, row, packed],
        out_shape=[jax.ShapeDtypeStruct((m, d), F32), jax.ShapeDtypeStruct((m, d), BF16),
                   jax.ShapeDtypeStruct((m * packs, LANES), I32)],
        compiler_params=_params(("arbitrary",)),
        name="residual_ln",
    )(h, f, g, b)


def _router_kernel(h_ref, w_ref, b_ref, idx_ref, rank_ref, wts_ref, cnt_ref, run_ref, *, tr, n_exp):
    @pl.when(pl.program_id(0) == 0)
    def _():
        run_ref[...] = jnp.zeros_like(run_ref)

    h = h_ref[...]
    w = w_ref[...]
    h_hi = h.astype(BF16)
    h_lo = (h - h_hi.astype(F32)).astype(BF16)
    w_hi = w.astype(BF16)
    w_lo = (w - w_hi.astype(F32)).astype(BF16)
    logits = _dot(h_hi, w_hi) + (_dot(h_hi, w_lo) + _dot(h_lo, w_hi))
    scores = jax.nn.sigmoid(logits)
    sel = scores + b_ref[...]
    lane = lax.broadcasted_iota(I32, (tr, n_exp), 1).astype(F32)
    out_lane = lax.broadcasted_iota(I32, (tr, IDX_LANES), 1)
    mask = jnp.zeros((tr, n_exp), F32)
    idx_out = jnp.zeros((tr, IDX_LANES), F32)
    picks = []
    for k in range(TOP_K):
        best = jnp.max(sel, axis=-1, keepdims=True)
        pick = jnp.min(jnp.where(sel == best, lane, float(n_exp)), axis=-1, keepdims=True)
        hit = lane == pick
        mask = jnp.where(hit, 1.0, mask)
        sel = jnp.where(hit, -jnp.inf, sel)
        idx_out = jnp.where(out_lane == k, pick, idx_out)
        picks.append(hit)
    chosen_scores = mask * scores
    gates = chosen_scores / jnp.sum(chosen_scores, axis=-1, keepdims=True) * ROUTED_SCALE

    rr = lax.broadcasted_iota(I32, (tr, tr), 0)
    cc = lax.broadcasted_iota(I32, (tr, tr), 1)
    lower = jnp.where(rr > cc, 1.0, 0.0).astype(BF16)
    rank = _dot(lower, mask.astype(BF16)) + run_ref[...]
    run_ref[...] = run_ref[...] + jnp.sum(mask, axis=0, keepdims=True)
    cnt_ref[...] = run_ref[...].astype(I32)

    rank_out = jnp.zeros((tr, IDX_LANES), F32)
    wts_out = jnp.zeros((tr, IDX_LANES), F32)
    for k in range(TOP_K):
        rank_out = jnp.where(out_lane == k, jnp.sum(jnp.where(picks[k], rank, 0.0), axis=-1, keepdims=True), rank_out)
        wts_out = jnp.where(out_lane == k, jnp.sum(jnp.where(picks[k], gates, 0.0), axis=-1, keepdims=True), wts_out)
    idx_ref[...] = idx_out.astype(I32)
    rank_ref[...] = rank_out.astype(I32)
    wts_ref[...] = wts_out


def _router(h, w_router, b_router3, layer, tr):
    m, d = h.shape
    n_exp = w_router.shape[-1]
    tr = min(tr, m)
    lists = pl.BlockSpec((tr, IDX_LANES), lambda i: (i, 0))
    return pl.pallas_call(
        functools.partial(_router_kernel, tr=tr, n_exp=n_exp),
        grid=(m // tr,),
        in_specs=[
            pl.BlockSpec((tr, d), lambda i: (i, 0)),
            pl.BlockSpec((None, d, n_exp), lambda i: (layer, 0, 0)),
            pl.BlockSpec((None, 1, n_exp), lambda i: (layer, 0, 0)),
        ],
        out_specs=[lists, lists, lists, pl.BlockSpec((1, n_exp), lambda i: (0, 0))],
        out_shape=[
            jax.ShapeDtypeStruct((m, IDX_LANES), I32),
            jax.ShapeDtypeStruct((m, IDX_LANES), I32),
            jax.ShapeDtypeStruct((m, IDX_LANES), F32),
            jax.ShapeDtypeStruct((1, n_exp), I32),
        ],
        scratch_shapes=[pltpu.VMEM((1, n_exp), F32)],
        compiler_params=_params(("arbitrary",)),
        name="router_topk",
    )(h, w_router, b_router3)


def _inverse_kernel(dest_ref, zeros_hbm, src_ref, sem, *, n_pairs):
    clear = pltpu.make_async_copy(zeros_hbm, src_ref, sem)
    clear.start()
    clear.wait()

    def place(t, c):
        for k in range(TOP_K):
            src_ref[dest_ref[t * TOP_K + k]] = t
        return c

    lax.fori_loop(0, n_pairs // TOP_K, place, 0, unroll=2)


def _inverse_map(dest_flat, n_rows):
    n_pairs = dest_flat.shape[0]
    smem = pl.BlockSpec(memory_space=pltpu.SMEM)
    return pl.pallas_call(
        functools.partial(_inverse_kernel, n_pairs=n_pairs),
        in_specs=[smem, pl.BlockSpec(memory_space=pl.ANY)],
        out_specs=smem,
        out_shape=jax.ShapeDtypeStruct((n_rows,), I32),
        scratch_shapes=[pltpu.SemaphoreType.DMA(())],
        name="inverse_map",
    )(dest_flat, jnp.zeros((n_rows,), I32))


def _route_plan_kernel(cnt_ref, idx_ref, rank_ref, dest_ref, te_ref, nu_ref, off_ref, *,
                       n_exp, tile, n_tiles):
    shift = tile.bit_length() - 1

    def per_expert(e, carry):
        tiles_before, last_owner = carry
        nt = lax.shift_right_logical(cnt_ref[e] + (tile - 1), shift)
        off_ref[e] = tiles_before * tile

        def fill(q, c):
            te_ref[tiles_before + q] = e
            return c

        lax.fori_loop(0, nt, fill, 0)
        return tiles_before + nt, jnp.where(nt > 0, e, last_owner)

    used, last_owner = lax.fori_loop(0, n_exp, per_expert, (jnp.int32(0), jnp.int32(0)))
    nu_ref[0] = used

    def fill_tail(q, c):
        te_ref[q] = last_owner
        return c

    lax.fori_loop(used, n_tiles, fill_tail, 0)

    idx = idx_ref[...]
    offset = lax.fori_loop(0, n_exp, lambda e, acc: jnp.where(idx == e, off_ref[e], acc),
                           jnp.zeros_like(idx))
    dest_ref[...] = offset + rank_ref[...]


def _route_plan(counts, idx, rank, tile, n_tiles):
    n_exp = counts.shape[-1]
    pairs = idx[:, :TOP_K].reshape(-1, LANES)
    ranks = rank[:, :TOP_K].reshape(-1, LANES)
    smem = pl.BlockSpec(memory_space=pltpu.SMEM)
    vmem = pl.BlockSpec(memory_space=pltpu.VMEM)
    assert tile & (tile - 1) == 0
    return pl.pallas_call(
        functools.partial(_route_plan_kernel, n_exp=n_exp, tile=tile, n_tiles=n_tiles),
        in_specs=[smem, vmem, vmem],
        out_specs=[vmem, smem, smem],
        out_shape=[jax.ShapeDtypeStruct(pairs.shape, I32), jax.ShapeDtypeStruct((n_tiles,), I32),
                   jax.ShapeDtypeStruct((1,), I32)],
        scratch_shapes=[pltpu.SMEM((n_exp,), I32)],
        name="route_plan",
    )(counts.reshape(n_exp), pairs, ranks)


def _expert_kernel(te_ref, nu_ref, src_ref, h_hbm, wg_ref, wu_ref, wd_ref, o_ref,
                   xbuf, x_bf, wg_bf, wu_bf, wd_bf, sem, *, tile):
    i = pl.program_id(0)
    n_used = nu_ref[0]
    slot = i % 2

    packs = x_bf.shape[1] // PACK_COLS
    pitch = xbuf.shape[0] // (2 * tile)

    def start_row(tile_idx, s, r):
        src_row = pl.multiple_of(src_ref[tile_idx * tile + r] * packs, packs)
        dst_row = pl.multiple_of((s * tile + r) * pitch, SUBLANES)
        pltpu.make_async_copy(h_hbm.at[pl.ds(src_row, packs)], xbuf.at[pl.ds(dst_row, packs)],
                              sem.at[s]).start()

    def wait_gather(s):
        dst_row = pl.multiple_of(s * (tile * pitch), SUBLANES)
        pltpu.make_async_copy(h_hbm.at[pl.ds(0, tile * packs)], xbuf.at[pl.ds(dst_row, tile * packs)],
                              sem.at[s]).wait()

    @pl.when(i == 0)
    def _():
        def body(r, c):
            start_row(0, 0, r)
            return c
        lax.fori_loop(0, tile, body, 0, unroll=8)

    def compute(prefetch_next):
        changed = jnp.logical_or(i == 0, te_ref[i] != te_ref[jnp.maximum(i - 1, 0)])

        @pl.when(changed)
        def _():
            _cast_rows_to_bf16(wg_ref, wg_bf)
            _cast_rows_to_bf16(wu_ref, wu_bf)
            _cast_rows_to_bf16(wd_ref, wd_bf)

        wait_gather(slot)
        rows_per_chunk = tile // packs
        for j in range(packs):
            if prefetch_next:
                for r in range(j * rows_per_chunk, (j + 1) * rows_per_chunk):
                    start_row(i + 1, 1 - slot, r)
            lo, hi = _unpack_bf16_pairs(xbuf[pl.ds(slot * (tile * pitch) + j, tile, stride=pitch), :])
            c0 = j * PACK_COLS
            x_bf[:, c0:c0 + LANES] = lo.astype(BF16)
            x_bf[:, c0 + LANES:c0 + PACK_COLS] = hi.astype(BF16)
        x = x_bf[...]
        act = jax.nn.silu(_dot(x, wg_bf[...])) * _dot(x, wu_bf[...])
        _store_packed_rows(_dot(act.astype(BF16), wd_bf[...]), o_ref)

    @pl.when(i + 1 < n_used)
    def _():
        compute(True)

    @pl.when(i + 1 == n_used)
    def _():
        compute(False)

    @pl.when(i >= n_used)
    def _():
        o_ref[...] = jnp.zeros_like(o_ref)


def _experts(tile_expert, n_used, src, h_pk, w_e_gate, w_e_up, w_e_down, layer, n_tiles, tile):
    d, ff = w_e_gate.shape[-2:]
    packs = d // PACK_COLS

    grid_spec = pltpu.PrefetchScalarGridSpec(
        num_scalar_prefetch=3,
        grid=(n_tiles,),
        in_specs=[
            pl.BlockSpec(memory_space=pl.ANY),
            pl.BlockSpec((None, None, d, ff), lambda i, te, nu, s: (layer, te[i], 0, 0)),
            pl.BlockSpec((None, None, d, ff), lambda i, te, nu, s: (layer, te[i], 0, 0)),
            pl.BlockSpec((None, None, ff, d), lambda i, te, nu, s: (layer, te[i], 0, 0)),
        ],
        out_specs=pl.BlockSpec((tile * packs, LANES), lambda i, te, nu, s: (i, 0)),
        scratch_shapes=[
            pltpu.VMEM((2 * tile * _gather_pitch(packs), LANES), I32),
            pltpu.VMEM((tile, d), BF16),
            pltpu.VMEM((d, ff), BF16),
            pltpu.VMEM((d, ff), BF16),
            pltpu.VMEM((ff, d), BF16),
            pltpu.SemaphoreType.DMA((2,)),
        ],
    )
    return pl.pallas_call(
        functools.partial(_expert_kernel, tile=tile),
        grid_spec=grid_spec,
        out_shape=jax.ShapeDtypeStruct((n_tiles * tile * packs, LANES), I32),
        compiler_params=_params(("arbitrary",)),
        name="expert_mlp",
    )(tile_expert, n_used, src, h_pk, w_e_gate, w_e_up, w_e_down)


def _swiglu_kernel(a_ref, wg_ref, wu_ref, o_ref, wg_bf, wu_bf):
    @pl.when(pl.program_id(1) == 0)
    def _():
        _cast_rows_to_bf16(wg_ref, wg_bf)
        _cast_rows_to_bf16(wu_ref, wu_bf)

    a = a_ref[...]
    o_ref[...] = (jax.nn.silu(_dot(a, wg_bf[...])) * _dot(a, wu_bf[...])).astype(o_ref.dtype)


def _swiglu(a, w_gate, w_up, layer, tm, tn):
    m, k = a.shape
    n = w_gate.shape[-1]
    tm = min(tm, m)
    tn = min(tn, n)
    wspec = pl.BlockSpec((None, k, tn), lambda j, i: (layer, 0, j), pipeline_mode=pl.Buffered(1))
    return pl.pallas_call(
        _swiglu_kernel,
        grid=(n // tn, m // tm),
        in_specs=[pl.BlockSpec((tm, k), lambda j, i: (i, 0)), wspec, wspec],
        out_specs=pl.BlockSpec((tm, tn), lambda j, i: (i, j)),
        out_shape=jax.ShapeDtypeStruct((m, n), BF16),
        scratch_shapes=[pltpu.VMEM((k, tn), BF16), pltpu.VMEM((k, tn), BF16)],
        compiler_params=_params(("arbitrary", "arbitrary")),
        name="shared_swiglu",
    )(a, w_gate, w_up)


def _combine_kernel(dest_ref, y_hbm, h_ref, sh_ref, wts_ref, g_ref, b_ref, o_ref, obf_ref,
                    ybuf, routed, sem, *, tc, alpha):
    i = pl.program_id(0)
    n_steps = pl.num_programs(0)
    slot = i % 2

    packs = routed.shape[1] // PACK_COLS
    pitch = ybuf.shape[0] // (2 * TOP_K * tc)

    def start_row(step, s, t, k):
        src_row = pl.multiple_of(dest_ref[(step * tc + t) * TOP_K + k] * packs, packs)
        dst_row = pl.multiple_of(((s * TOP_K + k) * tc + t) * pitch, SUBLANES)
        pltpu.make_async_copy(y_hbm.at[pl.ds(src_row, packs)], ybuf.at[pl.ds(dst_row, packs)],
                              sem.at[s]).start(priority=k % 2)

    def wait_gather(s):
        dst_row = pl.multiple_of(s * (TOP_K * tc * pitch), SUBLANES)
        pltpu.make_async_copy(y_hbm.at[pl.ds(0, TOP_K * tc * packs)],
                              ybuf.at[pl.ds(dst_row, TOP_K * tc * packs)], sem.at[s]).wait()

    @pl.when(i == 0)
    def _():
        def body(t, c):
            for k in range(TOP_K):
                start_row(0, 0, t, k)
            return c
        lax.fori_loop(0, tc, body, 0, unroll=2)

    def compute(prefetch_next):
        wait_gather(slot)
        wts = wts_ref[...]
        gate = [jnp.broadcast_to(wts[:, k:k + 1], (tc, LANES)) for k in range(TOP_K)]
        tokens_per_chunk = tc // packs
        for j in range(packs):
            if prefetch_next:
                for t in range(j * tokens_per_chunk, (j + 1) * tokens_per_chunk):
                    for k in range(TOP_K):
                        start_row(i + 1, 1 - slot, t, k)
            acc_lo = acc_hi = None
            for k in range(TOP_K):
                base = (slot * TOP_K + k) * (tc * pitch) + j
                lo, hi = _unpack_bf16_pairs(ybuf[pl.ds(base, tc, stride=pitch), :])
                acc_lo = gate[k] * lo if acc_lo is None else acc_lo + gate[k] * lo
                acc_hi = gate[k] * hi if acc_hi is None else acc_hi + gate[k] * hi
            c0 = j * PACK_COLS
            routed[:, c0:c0 + LANES] = acc_lo
            routed[:, c0 + LANES:c0 + PACK_COLS] = acc_hi
        y = alpha * h_ref[...] + (routed[...] + sh_ref[...])
        out = _layer_norm_rows(y, g_ref[...], b_ref[...])
        o_ref[...] = out
        obf_ref[...] = out.astype(BF16)

    @pl.when(i + 1 < n_steps)
    def _():
        compute(True)

    @pl.when(i + 1 == n_steps)
    def _():
        compute(False)


def _combine(dest_flat, y_sorted, h, shared, wts, g, b, layer, alpha, tc):
    m, d = h.shape
    tc = min(tc, m)
    row = pl.BlockSpec((tc, d), lambda i, dest: (i, 0))
    vec = pl.BlockSpec((None, 1, d), lambda i, dest: (layer, 0, 0))
    grid_spec = pltpu.PrefetchScalarGridSpec(
        num_scalar_prefetch=1,
        grid=(m // tc,),
        in_specs=[
            pl.BlockSpec(memory_space=pl.ANY),
            row, row,
            pl.BlockSpec((tc, IDX_LANES), lambda i, dest: (i, 0)),
            vec, vec,
        ],
        out_specs=[row, row],
        scratch_shapes=[pltpu.VMEM((2 * TOP_K * tc * _gather_pitch(d // PACK_COLS), LANES), I32),
                        pltpu.VMEM((tc, d), F32), pltpu.SemaphoreType.DMA((2,))],
    )
    return pl.pallas_call(
        functools.partial(_combine_kernel, tc=tc, alpha=alpha),
        grid_spec=grid_spec,
        out_shape=[jax.ShapeDtypeStruct((m, d), F32), jax.ShapeDtypeStruct((m, d), BF16)],
        compiler_params=_params(("arbitrary",)),
        name="moe_combine_ln",
    )(dest_flat, y_sorted, h, shared, wts, g, b)


def kernel(x, w_in, w_gate, b_gate, w_branch, w_out, gm_ln_g, gm_ln_b, w_spatial, b_spatial, conv_w,
           ln1_g, ln1_b, w_router, b_router, w_e_gate, w_e_up, w_e_down, w_s_gate, w_s_up, w_s_down,
           ln2_g, ln2_b):
    batch, seq, d = x.shape
    depth = w_in.shape[0]
    m = batch * seq
    bw = d // 4
    heads = bw // HEAD_DIM
    n_exp = w_router.shape[-1]
    alpha = float((2 * depth) ** 0.25)
    tile = EXPERT_ROW_TILE
    n_tiles = (m * TOP_K + n_exp * (tile - 1)) // tile + 1
    n_rows = n_tiles * tile

    b_gate4 = b_gate.reshape(depth, N_BRANCH, 1, d)
    gm_g3 = gm_ln_g.reshape(depth, 1, bw)
    gm_b3 = gm_ln_b.reshape(depth, 1, bw)
    b_sp_bcast = jnp.broadcast_to(b_spatial[..., None], b_spatial.shape + (GM_SPAN,))
    ln1_g3, ln1_b3 = ln1_g.reshape(depth, 1, d), ln1_b.reshape(depth, 1, d)
    ln2_g3, ln2_b3 = ln2_g.reshape(depth, 1, d), ln2_b.reshape(depth, 1, d)
    b_router3 = b_router.reshape(depth, 1, n_exp)

    h = x.reshape(m, d)
    h_bf = h.astype(BF16)
    for l in range(depth):
        qkv = _matmul(h_bf, w_in, l, 0, 3 * bw, 512, 1024, BF16, "proj_qkv")
        rest = _matmul(h_bf, w_in, l, 3 * bw, 5 * bw, 512, 1024, F32, "proj_rest")
        o_a = _attention(qkv, batch, seq, heads, 512, 512)
        o_bc = _gmlp_conv(rest, gm_g3, gm_b3, w_spatial, b_sp_bcast, conv_w, l, seq, bw, 256)
        merged = _merge(h_bf, o_a, o_bc, w_gate, b_gate4, w_branch, l, 512, 256)
        mix = _matmul(merged, w_out, l, 0, d, 512, 1024, F32, "out_proj")
        h, h_bf, h_pk = _residual_ln(h, mix, ln1_g3, ln1_b3, l, alpha, 256)

        idx, rank, wts, counts = _router(h, w_router, b_router3, l, 512)
        dest2d, tile_expert, n_used = _route_plan(counts, idx, rank, tile, n_tiles)
        dest = dest2d.reshape(-1)
        src = _inverse_map(dest, n_rows)
        y_sorted = _experts(tile_expert, n_used, src, h_pk, w_e_gate, w_e_up, w_e_down, l, n_tiles, tile)
        act_s = _swiglu(h_bf, w_s_gate, w_s_up, l, 512, 512)
        shared = _matmul(act_s, w_s_down, l, 0, d, 512, 1024, F32, "shared_down")
        h, h_bf = _combine(dest, y_sorted, h, shared, wts, ln2_g3, ln2_b3, l, alpha, 128)
    return h.reshape(batch, seq, d)
```

```python
import functools

import jax
import jax.numpy as jnp
from jax import lax
from jax.experimental import pallas as pl
from jax.experimental.pallas import tpu as pltpu

F32 = jnp.float32
BF16 = jnp.bfloat16
I32 = jnp.int32

CHUNK = 64
HEAD_DIM = 128
GM_SPAN = 128
GM_GROUP_DIM = 128
IN_PARTS = 8
N_BRANCH = 3
TOP_K = 6
ROUTED_SCALE = 2.5
LN_EPS = 1e-5
LOG2_E = 1.4426950408889634

V7X_VMEM_LIMIT_BYTES = 56 * 1024 * 1024
LANES = 128
SUBLANES = 8
CAST_ROWS = 256
EXPERT_ROW_TILE = 256
IDX_LANES = 8
MATMUL_TILE = (1024, 512)
SWIGLU_TILE = (1024, 256)
MERGE_TILE = (512, 256)
ATTN_TILE = (512, 512, 2)
TOKEN_TILE = 256
ROUTER_TILE = 512
COMBINE_TILE = 128


def _params(sem):
    return pltpu.CompilerParams(dimension_semantics=sem, vmem_limit_bytes=V7X_VMEM_LIMIT_BYTES)


def _cast_rows_to_bf16(src_ref, dst_ref):
    k = src_ref.shape[0]
    rows = min(CAST_ROWS, k)

    def body(i, c):
        r = pl.multiple_of(i * rows, rows)
        dst_ref[pl.ds(r, rows), :] = src_ref[pl.ds(r, rows), :].astype(BF16)
        return c

    lax.fori_loop(0, k // rows, body, 0)


def _dot(a, b):
    return jnp.dot(a, b, preferred_element_type=F32)


PACK_COLS = 2 * LANES
HIGH_HALF = -65536


def _pack_bf16_pairs(lo, hi):
    lo_bits = lax.bitcast_convert_type(lo.astype(BF16).astype(F32), I32)
    hi_bits = lax.bitcast_convert_type(hi.astype(BF16).astype(F32), I32)
    return hi_bits | lax.shift_right_logical(lo_bits, 16)


def _unpack_bf16_pairs(word):
    lo = lax.bitcast_convert_type(word << 16, F32)
    hi = lax.bitcast_convert_type(word & HIGH_HALF, F32)
    return lo, hi


def _gather_pitch(packs):
    pitch = -(-packs // SUBLANES) * SUBLANES
    return pitch + SUBLANES if pitch % (2 * SUBLANES) == 0 else pitch


def _store_packed_rows(values, pk_ref):
    rows, cols = values.shape
    packs = cols // PACK_COLS
    for j in range(packs):
        c0 = j * PACK_COLS
        word = _pack_bf16_pairs(values[:, c0:c0 + LANES], values[:, c0 + LANES:c0 + PACK_COLS])
        pk_ref[pl.ds(j, rows, stride=packs), :] = word


def _mm_kernel(a_ref, w_ref, o_ref, wbf_ref):
    @pl.when(pl.program_id(1) == 0)
    def _():
        _cast_rows_to_bf16(w_ref, wbf_ref)

    o_ref[...] = _dot(a_ref[...], wbf_ref[...]).astype(o_ref.dtype)


def _matmul(a, w, layer, col_off, n_cols, tm, tn, out_dtype, name):
    m, k = a.shape
    tm = min(tm, m)
    while n_cols % tn or col_off % tn:
        tn //= 2
    col_block_off = col_off // tn
    return pl.pallas_call(
        _mm_kernel,
        grid=(n_cols // tn, m // tm),
        in_specs=[
            pl.BlockSpec((tm, k), lambda n, i: (i, 0)),
            pl.BlockSpec((None, k, tn), lambda n, i: (layer, 0, n + col_block_off)),
        ],
        out_specs=pl.BlockSpec((tm, tn), lambda n, i: (i, n)),
        out_shape=jax.ShapeDtypeStruct((m, n_cols), out_dtype),
        scratch_shapes=[pltpu.VMEM((k, tn), BF16)],
        compiler_params=_params(("arbitrary", "arbitrary")),
        name=name,
    )(a, w)


def _attn_kernel(q_ref, k_ref, v_ref, o_ref, *, tq, kb, ks, hp, scale):
    qi = pl.program_id(2)
    n_sub = ks // kb
    n_steps = ((qi + 1) * tq + ks - 1) // ks
    t_pos = lax.broadcasted_iota(I32, (tq, kb), 0) + qi * tq
    s_loc = lax.broadcasted_iota(I32, (tq, kb), 1)
    r = lax.broadcasted_iota(I32, (2 * kb, 2 * kb), 0)
    c = lax.broadcasted_iota(I32, (2 * kb, 2 * kb), 1)
    r = jnp.where(r >= kb, r - kb, r)
    suffix = jnp.where((c >= kb) | (r > c), 1.0, 0.0).astype(BF16)

    def step(j, carry, masked):
        k0 = pl.multiple_of(j * ks, ks)
        return tuple(head_step(h, k0, carry[h], masked) for h in range(hp))

    def head_step(h, k0, carry, masked):
        acc, later_sum = carry
        cols = slice(h * HEAD_DIM, (h + 1) * HEAD_DIM)
        q = q_ref[:, cols]
        kj = k_ref[pl.ds(k0, ks), cols]
        vj = v_ref[pl.ds(k0, ks), cols]
        z = lax.dot_general(q, kj, (((1,), (1,)), ((), ())), preferred_element_type=F32) * (scale * LOG2_E)
        weights = [None] * n_sub
        for s in reversed(range(n_sub)):
            zs = z[:, s * kb:(s + 1) * kb]
            softplus = jnp.log(1.0 + jnp.exp2(jnp.minimum(zs, -zs))) * LOG2_E
            log_beta = jnp.minimum(zs, 0.0) - softplus
            log_one_minus = log_beta - zs
            if masked:
                strict = (s_loc + (k0 + s * kb)) < t_pos
                log_one_minus = jnp.where(strict, log_one_minus, 0.0)
            hi = log_one_minus.astype(BF16)
            lo = (log_one_minus - hi.astype(F32)).astype(BF16)
            sums = _dot(jnp.concatenate([hi, lo], axis=1), suffix)
            a = jnp.exp2(log_beta + (sums[:, :kb] + later_sum))
            if masked:
                a = jnp.where(strict, a, 0.0)
            weights[s] = a.astype(BF16)
            later_sum = later_sum + sums[:, kb:]
        acc = acc + _dot(jnp.concatenate(weights, axis=1), vj)
        return acc, later_sum

    carry = tuple((jnp.zeros((tq, HEAD_DIM), F32), jnp.zeros((tq, kb), F32)) for _ in range(hp))
    carry = step(n_steps - 1, carry, True)
    carry = lax.fori_loop(0, n_steps - 1, lambda t, cr: step(n_steps - 2 - t, cr, False), carry)
    for h in range(hp):
        o_ref[:, h * HEAD_DIM:(h + 1) * HEAD_DIM] = carry[h][0].astype(o_ref.dtype)


def _attention(qkv, batch, seq, heads, tq, ks, hp):
    m = qkv.shape[0]
    tq = min(tq, seq)
    ks = min(ks, seq)
    hp = min(hp, heads)
    assert ks % tq == 0 and heads % hp == 0
    kb = HEAD_DIM
    nq = seq // tq
    groups = heads // hp
    width = hp * HEAD_DIM
    kern = functools.partial(_attn_kernel, tq=tq, kb=kb, ks=ks, hp=hp, scale=HEAD_DIM ** -0.5)
    return pl.pallas_call(
        kern,
        grid=(batch, groups, nq),
        in_specs=[
            pl.BlockSpec((tq, width), lambda b, g, i: (b * nq + i, g)),
            pl.BlockSpec((seq, width), lambda b, g, i: (b, groups + g)),
            pl.BlockSpec((seq, width), lambda b, g, i: (b, 2 * groups + g)),
        ],
        out_specs=pl.BlockSpec((tq, width), lambda b, g, i: (b * nq + i, g)),
        out_shape=jax.ShapeDtypeStruct((m, heads * HEAD_DIM), BF16),
        compiler_params=_params(("arbitrary", "arbitrary", "arbitrary")),
        name="sb_attention",
    )(qkv, qkv, qkv)


def _gelu_tanh(x):
    return 0.5 * x * (1.0 + jnp.tanh(0.7978845608028654 * (x + 0.044715 * (x * x * x))))


def _gmlp_conv_kernel(gu_ref, gv_ref, cb_ref, cc_ref, cx_ref, ccp_ref, cxp_ref, lng_ref, lnb_ref,
                      wsp_ref, bsp_ref, cw_ref, o_ref, *, tt, seq, groups):
    i = pl.program_id(0)
    z = cc_ref[...] * cx_ref[...]
    first = (i * tt) % seq == 0
    zp = jnp.where(first, 0.0, ccp_ref[...] * cxp_ref[...])
    row = lax.broadcasted_iota(I32, z.shape, 0)
    z1 = jnp.where(row == 0, zp[SUBLANES - 1:SUBLANES, :], pltpu.roll(z, 1, 0))
    z2 = pltpu.roll(z, 2, 0)
    z2 = jnp.where(row == 0, zp[SUBLANES - 2:SUBLANES - 1, :], z2)
    z2 = jnp.where(row == 1, zp[SUBLANES - 1:SUBLANES, :], z2)
    y = cw_ref[0:1, :] * z2 + cw_ref[1:2, :] * z1 + cw_ref[2:3, :] * z
    o_ref[1] = (cb_ref[...] * y).astype(o_ref.dtype)

    v = _gelu_tanh(gv_ref[...])
    mu = jnp.mean(v, axis=-1, keepdims=True)
    var = jnp.mean(jnp.square(v - mu), axis=-1, keepdims=True)
    vn = ((v - mu) * lax.rsqrt(var + LN_EPS) * lng_ref[...] + lnb_ref[...]).astype(BF16)
    u = _gelu_tanh(gu_ref[...])
    tr = lax.broadcasted_iota(I32, (GM_SPAN, GM_SPAN), 0) // CHUNK
    sc = lax.broadcasted_iota(I32, (GM_SPAN, GM_SPAN), 1) // CHUNK
    causal = tr >= sc
    for g in range(groups):
        wm = jnp.where(causal, wsp_ref[g], 0.0).astype(BF16)
        cols = slice(g * GM_GROUP_DIM, (g + 1) * GM_GROUP_DIM)
        for s in range(tt // GM_SPAN):
            rows = slice(s * GM_SPAN, (s + 1) * GM_SPAN)
            f = _dot(wm, vn[rows, cols]) + bsp_ref[g]
            o_ref[0, rows, cols] = (u[rows, cols] * f).astype(o_ref.dtype)


def _gmlp_conv(rest, gm_ln_g, gm_ln_b, w_spatial, b_spatial_bcast, conv_w, layer, seq, bw, tt):
    m = rest.shape[0]
    tt = min(tt, seq)
    groups = bw // GM_GROUP_DIM
    halo_blocks = tt // SUBLANES
    kern = functools.partial(_gmlp_conv_kernel, tt=tt, seq=seq, groups=groups)

    def part(p):
        return pl.BlockSpec((tt, bw), lambda i: (i, p))

    def halo(p):
        return pl.BlockSpec((SUBLANES, bw), lambda i: (jnp.maximum(i * halo_blocks - 1, 0), p))

    return pl.pallas_call(
        kern,
        grid=(m // tt,),
        in_specs=[
            part(0), part(1), part(2), part(3), part(4), halo(3), halo(4),
            pl.BlockSpec((None, 1, bw), lambda i: (layer, 0, 0)),
            pl.BlockSpec((None, 1, bw), lambda i: (layer, 0, 0)),
            pl.BlockSpec((None, groups, GM_SPAN, GM_SPAN), lambda i: (layer, 0, 0, 0)),
            pl.BlockSpec((None, groups, GM_SPAN, GM_SPAN), lambda i: (layer, 0, 0, 0)),
            pl.BlockSpec((None, 3, bw), lambda i: (layer, 0, 0)),
        ],
        out_specs=pl.BlockSpec((2, tt, bw), lambda i: (0, i, 0)),
        out_shape=jax.ShapeDtypeStruct((2, m, bw), BF16),
        compiler_params=_params(("arbitrary",)),
        name="gmlp_conv",
    )(rest, rest, rest, rest, rest, rest, rest, gm_ln_g, gm_ln_b, w_spatial, b_spatial_bcast, conv_w)


def _merge_kernel(h_ref, oa_ref, obc_ref, wg_ref, bg_ref, wb_ref, o_ref, wg_bf, wb_bf):
    @pl.when(pl.program_id(1) == 0)
    def _():
        for i in range(N_BRANCH):
            _cast_rows_to_bf16(wg_ref.at[i], wg_bf.at[i])
            _cast_rows_to_bf16(wb_ref.at[i], wb_bf.at[i])

    h = h_ref[...]
    merged = None
    for i in range(N_BRANCH):
        gate = jax.nn.sigmoid(_dot(h, wg_bf[i]) + bg_ref[i])
        branch = oa_ref[...] if i == 0 else obc_ref[i - 1]
        term = gate * _dot(branch, wb_bf[i])
        merged = term if merged is None else merged + term
    o_ref[...] = merged.astype(o_ref.dtype)


def _merge(h_bf, o_a, o_bc, w_gate, b_gate4, w_branch, layer, tm, tn):
    m, d = h_bf.shape
    bw = o_a.shape[1]
    tm = min(tm, m)
    tn = min(tn, d)
    return pl.pallas_call(
        _merge_kernel,
        grid=(d // tn, m // tm),
        in_specs=[
            pl.BlockSpec((tm, d), lambda n, i: (i, 0)),
            pl.BlockSpec((tm, bw), lambda n, i: (i, 0)),
            pl.BlockSpec((2, tm, bw), lambda n, i: (0, i, 0)),
            pl.BlockSpec((None, N_BRANCH, d, tn), lambda n, i: (layer, 0, 0, n),
                         pipeline_mode=pl.Buffered(1)),
            pl.BlockSpec((None, N_BRANCH, 1, tn), lambda n, i: (layer, 0, 0, n)),
            pl.BlockSpec((None, N_BRANCH, bw, tn), lambda n, i: (layer, 0, 0, n),
                         pipeline_mode=pl.Buffered(1)),
        ],
        out_specs=pl.BlockSpec((tm, tn), lambda n, i: (i, n)),
        out_shape=jax.ShapeDtypeStruct((m, d), BF16),
        scratch_shapes=[pltpu.VMEM((N_BRANCH, d, tn), BF16), pltpu.VMEM((N_BRANCH, bw, tn), BF16)],
        compiler_params=_params(("arbitrary", "arbitrary")),
        name="branch_merge",
    )(h_bf, o_a, o_bc, w_gate, b_gate4, w_branch)


def _layer_norm_rows(y, g, b):
    mu = jnp.mean(y, axis=-1, keepdims=True)
    var = jnp.mean(jnp.square(y - mu), axis=-1, keepdims=True)
    return (y - mu) * lax.rsqrt(var + LN_EPS) * g + b


def _residual_ln_kernel(h_ref, f_ref, g_ref, b_ref, o_ref, obf_ref, opk_ref, *, alpha):
    out = _layer_norm_rows(alpha * h_ref[...] + f_ref[...], g_ref[...], b_ref[...])
    o_ref[...] = out
    obf_ref[...] = out.astype(BF16)
    _store_packed_rows(out, opk_ref)


def _residual_ln(h, f, g, b, layer, alpha, tm):
    m, d = h.shape
    tm = min(tm, m)
    row = pl.BlockSpec((tm, d), lambda i: (i, 0))
    vec = pl.BlockSpec((None, 1, d), lambda i: (layer, 0, 0))
    packs = d // PACK_COLS
    packed = pl.BlockSpec((tm * packs, LANES), lambda i: (i, 0))
    return pl.pallas_call(
        functools.partial(_residual_ln_kernel, alpha=alpha),
        grid=(m // tm,),
        in_specs=[row, row, vec, vec],
        out_specs=[row, row, packed],
        out_shape=[jax.ShapeDtypeStruct((m, d), F32), jax.ShapeDtypeStruct((m, d), BF16),
                   jax.ShapeDtypeStruct((m * packs, LANES), I32)],
        compiler_params=_params(("arbitrary",)),
        name="residual_ln",
    )(h, f, g, b)


def _router_kernel(h_ref, w_ref, b_ref, idx_ref, rank_ref, wts_ref, cnt_ref, run_ref, *, tr, n_exp):
    @pl.when(pl.program_id(0) == 0)
    def _():
        run_ref[...] = jnp.zeros_like(run_ref)

    h = h_ref[...]
    w = w_ref[...]
    h_hi = h.astype(BF16)
    h_lo = (h - h_hi.astype(F32)).astype(BF16)
    w_hi = w.astype(BF16)
    w_lo = (w - w_hi.astype(F32)).astype(BF16)
    logits = _dot(h_hi, w_hi) + (_dot(h_hi, w_lo) + _dot(h_lo, w_hi))
    scores = jax.nn.sigmoid(logits)
    sel = scores + b_ref[...]
    lane = lax.broadcasted_iota(I32, (tr, n_exp), 1).astype(F32)
    out_lane = lax.broadcasted_iota(I32, (tr, IDX_LANES), 1)
    mask = jnp.zeros((tr, n_exp), F32)
    idx_out = jnp.zeros((tr, IDX_LANES), F32)
    picks = []
    for k in range(TOP_K):
        best = jnp.max(sel, axis=-1, keepdims=True)
        pick = jnp.min(jnp.where(sel == best, lane, float(n_exp)), axis=-1, keepdims=True)
        hit = lane == pick
        mask = jnp.where(hit, 1.0, mask)
        sel = jnp.where(hit, -jnp.inf, sel)
        idx_out = jnp.where(out_lane == k, pick, idx_out)
        picks.append(hit)
    chosen_scores = mask * scores
    gates = chosen_scores / jnp.sum(chosen_scores, axis=-1, keepdims=True) * ROUTED_SCALE

    rr = lax.broadcasted_iota(I32, (tr, tr), 0)
    cc = lax.broadcasted_iota(I32, (tr, tr), 1)
    lower = jnp.where(rr > cc, 1.0, 0.0).astype(BF16)
    rank = _dot(lower, mask.astype(BF16)) + run_ref[...]
    run_ref[...] = run_ref[...] + jnp.sum(mask, axis=0, keepdims=True)
    cnt_ref[...] = run_ref[...].astype(I32)

    rank_out = jnp.zeros((tr, IDX_LANES), F32)
    wts_out = jnp.zeros((tr, IDX_LANES), F32)
    for k in range(TOP_K):
        rank_out = jnp.where(out_lane == k, jnp.sum(jnp.where(picks[k], rank, 0.0), axis=-1, keepdims=True), rank_out)
        wts_out = jnp.where(out_lane == k, jnp.sum(jnp.where(picks[k], gates, 0.0), axis=-1, keepdims=True), wts_out)
    idx_ref[...] = idx_out.astype(I32)
    rank_ref[...] = rank_out.astype(I32)
    wts_ref[...] = wts_out


def _router(h, w_router, b_router3, layer, tr):
    m, d = h.shape
    n_exp = w_router.shape[-1]
    tr = min(tr, m)
    lists = pl.BlockSpec((tr, IDX_LANES), lambda i: (i, 0))
    return pl.pallas_call(
        functools.partial(_router_kernel, tr=tr, n_exp=n_exp),
        grid=(m // tr,),
        in_specs=[
            pl.BlockSpec((tr, d), lambda i: (i, 0)),
            pl.BlockSpec((None, d, n_exp), lambda i: (layer, 0, 0)),
            pl.BlockSpec((None, 1, n_exp), lambda i: (layer, 0, 0)),
        ],
        out_specs=[lists, lists, lists, pl.BlockSpec((1, n_exp), lambda i: (0, 0))],
        out_shape=[
            jax.ShapeDtypeStruct((m, IDX_LANES), I32),
            jax.ShapeDtypeStruct((m, IDX_LANES), I32),
            jax.ShapeDtypeStruct((m, IDX_LANES), F32),
            jax.ShapeDtypeStruct((1, n_exp), I32),
        ],
        scratch_shapes=[pltpu.VMEM((1, n_exp), F32)],
        compiler_params=_params(("arbitrary",)),
        name="router_topk",
    )(h, w_router, b_router3)


def _inverse_kernel(dest_ref, zeros_hbm, src_ref, sem, *, n_pairs):
    clear = pltpu.make_async_copy(zeros_hbm, src_ref, sem)
    clear.start()
    clear.wait()

    def place(t, c):
        for k in range(TOP_K):
            src_ref[dest_ref[t * TOP_K + k]] = t
        return c

    lax.fori_loop(0, n_pairs // TOP_K, place, 0, unroll=2)


def _inverse_map(dest_flat, n_rows):
    n_pairs = dest_flat.shape[0]
    smem = pl.BlockSpec(memory_space=pltpu.SMEM)
    return pl.pallas_call(
        functools.partial(_inverse_kernel, n_pairs=n_pairs),
        in_specs=[smem, pl.BlockSpec(memory_space=pl.ANY)],
        out_specs=smem,
        out_shape=jax.ShapeDtypeStruct((n_rows,), I32),
        scratch_shapes=[pltpu.SemaphoreType.DMA(())],
        name="inverse_map",
    )(dest_flat, jnp.zeros((n_rows,), I32))


def _route_plan_kernel(cnt_ref, idx_ref, rank_ref, dest_ref, te_ref, nu_ref, nxt_ref, off_ref, *,
                       n_exp, tile, n_tiles):
    shift = tile.bit_length() - 1

    def link(t, following):
        e = n_exp - 1 - t
        nxt_ref[e] = following
        return jnp.where(cnt_ref[e] > 0, e, following)

    lax.fori_loop(0, n_exp, link, jnp.int32(-1))

    def per_expert(e, carry):
        tiles_before, last_owner = carry
        nt = lax.shift_right_logical(cnt_ref[e] + (tile - 1), shift)
        off_ref[e] = tiles_before * tile

        def fill(q, c):
            te_ref[tiles_before + q] = e
            return c

        lax.fori_loop(0, nt, fill, 0)
        return tiles_before + nt, jnp.where(nt > 0, e, last_owner)

    used, last_owner = lax.fori_loop(0, n_exp, per_expert, (jnp.int32(0), jnp.int32(0)))
    nu_ref[0] = used

    def fill_tail(q, c):
        te_ref[q] = last_owner
        return c

    lax.fori_loop(used, n_tiles, fill_tail, 0)

    idx = idx_ref[...]
    offset = lax.fori_loop(0, n_exp, lambda e, acc: jnp.where(idx == e, off_ref[e], acc),
                           jnp.zeros_like(idx))
    dest_ref[...] = offset + rank_ref[...]


def _route_plan(counts, idx, rank, tile, n_tiles):
    n_exp = counts.shape[-1]
    pairs = idx[:, :TOP_K].reshape(-1, LANES)
    ranks = rank[:, :TOP_K].reshape(-1, LANES)
    smem = pl.BlockSpec(memory_space=pltpu.SMEM)
    vmem = pl.BlockSpec(memory_space=pltpu.VMEM)
    assert tile & (tile - 1) == 0
    return pl.pallas_call(
        functools.partial(_route_plan_kernel, n_exp=n_exp, tile=tile, n_tiles=n_tiles),
        in_specs=[smem, vmem, vmem],
        out_specs=[vmem, smem, smem, smem],
        out_shape=[jax.ShapeDtypeStruct(pairs.shape, I32), jax.ShapeDtypeStruct((n_tiles,), I32),
                   jax.ShapeDtypeStruct((1,), I32), jax.ShapeDtypeStruct((n_exp,), I32)],
        scratch_shapes=[pltpu.SMEM((n_exp,), I32)],
        name="route_plan",
    )(counts.reshape(n_exp), pairs, ranks)


def _expert_kernel(te_ref, nu_ref, src_ref, nxt_ref, h_hbm, wg_hbm, wu_hbm, wd_hbm, o_ref,
                   xbuf, x_bf, wg_f32, wu_f32, wd_f32, wg_bf, wu_bf, wd_bf, wslot, sem, wsem,
                   *, tile, layer):
    i = pl.program_id(0)
    n_used = nu_ref[0]
    slot = i % 2

    packs = x_bf.shape[1] // PACK_COLS
    pitch = xbuf.shape[0] // (2 * tile)

    def weight_copies(expert, s):
        return [pltpu.make_async_copy(hbm.at[layer, expert], buf.at[s], wsem.at[s, n])
                for n, (hbm, buf) in enumerate(((wg_hbm, wg_f32), (wu_hbm, wu_f32), (wd_hbm, wd_f32)))]

    def start_row(tile_idx, s, r):
        src_row = pl.multiple_of(src_ref[tile_idx * tile + r] * packs, packs)
        dst_row = pl.multiple_of((s * tile + r) * pitch, SUBLANES)
        pltpu.make_async_copy(h_hbm.at[pl.ds(src_row, packs)], xbuf.at[pl.ds(dst_row, packs)],
                              sem.at[s]).start()

    def wait_gather(s):
        dst_row = pl.multiple_of(s * (tile * pitch), SUBLANES)
        pltpu.make_async_copy(h_hbm.at[pl.ds(0, tile * packs)], xbuf.at[pl.ds(dst_row, tile * packs)],
                              sem.at[s]).wait()

    @pl.when(i == 0)
    def _():
        wslot[0] = 0
        for cp in weight_copies(te_ref[0], 0):
            cp.start(priority=1)

        def body(r, c):
            start_row(0, 0, r)
            return c
        lax.fori_loop(0, tile, body, 0, unroll=8)

    def compute(prefetch_next):
        expert = te_ref[i]
        changed = jnp.logical_or(i == 0, expert != te_ref[jnp.maximum(i - 1, 0)])

        @pl.when(changed)
        def _():
            s = wslot[0]
            for cp in weight_copies(expert, s):
                cp.wait()
            following = nxt_ref[expert]

            @pl.when(following >= 0)
            def _():
                for cp in weight_copies(following, 1 - s):
                    cp.start(priority=1)

            _cast_rows_to_bf16(wg_f32.at[s], wg_bf)
            _cast_rows_to_bf16(wu_f32.at[s], wu_bf)
            _cast_rows_to_bf16(wd_f32.at[s], wd_bf)
            wslot[0] = 1 - s

        wait_gather(slot)
        rows_per_chunk = tile // packs
        for j in range(packs):
            if prefetch_next:
                for r in range(j * rows_per_chunk, (j + 1) * rows_per_chunk):
                    start_row(i + 1, 1 - slot, r)
            lo, hi = _unpack_bf16_pairs(xbuf[pl.ds(slot * (tile * pitch) + j, tile, stride=pitch), :])
            c0 = j * PACK_COLS
            x_bf[:, c0:c0 + LANES] = lo.astype(BF16)
            x_bf[:, c0 + LANES:c0 + PACK_COLS] = hi.astype(BF16)
        x = x_bf[...]
        act = jax.nn.silu(_dot(x, wg_bf[...])) * _dot(x, wu_bf[...])
        _store_packed_rows(_dot(act.astype(BF16), wd_bf[...]), o_ref)

    @pl.when(i + 1 < n_used)
    def _():
        compute(True)

    @pl.when(i + 1 == n_used)
    def _():
        compute(False)

    @pl.when(i >= n_used)
    def _():
        o_ref[...] = jnp.zeros_like(o_ref)


def _experts(tile_expert, n_used, src, next_used, h_pk, w_e_gate, w_e_up, w_e_down, layer, n_tiles, tile):
    d, ff = w_e_gate.shape[-2:]
    packs = d // PACK_COLS
    hbm = pl.BlockSpec(memory_space=pl.ANY)

    grid_spec = pltpu.PrefetchScalarGridSpec(
        num_scalar_prefetch=4,
        grid=(n_tiles,),
        in_specs=[hbm, hbm, hbm, hbm],
        out_specs=pl.BlockSpec((tile * packs, LANES), lambda i, te, nu, s, nx: (i, 0)),
        scratch_shapes=[
            pltpu.VMEM((2 * tile * _gather_pitch(packs), LANES), I32),
            pltpu.VMEM((tile, d), BF16),
            pltpu.VMEM((2, d, ff), F32),
            pltpu.VMEM((2, d, ff), F32),
            pltpu.VMEM((2, ff, d), F32),
            pltpu.VMEM((d, ff), BF16),
            pltpu.VMEM((d, ff), BF16),
            pltpu.VMEM((ff, d), BF16),
            pltpu.SMEM((1,), I32),
            pltpu.SemaphoreType.DMA((2,)),
            pltpu.SemaphoreType.DMA((2, 3)),
        ],
    )
    return pl.pallas_call(
        functools.partial(_expert_kernel, tile=tile, layer=layer),
        grid_spec=grid_spec,
        out_shape=jax.ShapeDtypeStruct((n_tiles * tile * packs, LANES), I32),
        compiler_params=_params(("arbitrary",)),
        name="expert_mlp",
    )(tile_expert, n_used, src, next_used, h_pk, w_e_gate, w_e_up, w_e_down)


def _swiglu_kernel(a_ref, wg_ref, wu_ref, o_ref, wg_bf, wu_bf):
    @pl.when(pl.program_id(1) == 0)
    def _():
        _cast_rows_to_bf16(wg_ref, wg_bf)
        _cast_rows_to_bf16(wu_ref, wu_bf)

    a = a_ref[...]
    o_ref[...] = (jax.nn.silu(_dot(a, wg_bf[...])) * _dot(a, wu_bf[...])).astype(o_ref.dtype)


def _swiglu(a, w_gate, w_up, layer, tm, tn):
    m, k = a.shape
    n = w_gate.shape[-1]
    tm = min(tm, m)
    tn = min(tn, n)
    wspec = pl.BlockSpec((None, k, tn), lambda j, i: (layer, 0, j))
    return pl.pallas_call(
        _swiglu_kernel,
        grid=(n // tn, m // tm),
        in_specs=[pl.BlockSpec((tm, k), lambda j, i: (i, 0)), wspec, wspec],
        out_specs=pl.BlockSpec((tm, tn), lambda j, i: (i, j)),
        out_shape=jax.ShapeDtypeStruct((m, n), BF16),
        scratch_shapes=[pltpu.VMEM((k, tn), BF16), pltpu.VMEM((k, tn), BF16)],
        compiler_params=_params(("arbitrary", "arbitrary")),
        name="shared_swiglu",
    )(a, w_gate, w_up)


def _combine_kernel(dest_ref, y_hbm, h_ref, sh_ref, wts_ref, g_ref, b_ref, o_ref, obf_ref,
                    ybuf, routed, sem, *, tc, alpha):
    i = pl.program_id(0)
    n_steps = pl.num_programs(0)
    slot = i % 2

    packs = routed.shape[1] // PACK_COLS
    pitch = ybuf.shape[0] // (2 * TOP_K * tc)

    def start_row(step, s, t, k):
        src_row = pl.multiple_of(dest_ref[(step * tc + t) * TOP_K + k] * packs, packs)
        dst_row = pl.multiple_of(((s * TOP_K + k) * tc + t) * pitch, SUBLANES)
        pltpu.make_async_copy(y_hbm.at[pl.ds(src_row, packs)], ybuf.at[pl.ds(dst_row, packs)],
                              sem.at[s]).start(priority=k % 2)

    def wait_gather(s):
        dst_row = pl.multiple_of(s * (TOP_K * tc * pitch), SUBLANES)
        pltpu.make_async_copy(y_hbm.at[pl.ds(0, TOP_K * tc * packs)],
                              ybuf.at[pl.ds(dst_row, TOP_K * tc * packs)], sem.at[s]).wait()

    @pl.when(i == 0)
    def _():
        def body(t, c):
            for k in range(TOP_K):
                start_row(0, 0, t, k)
            return c
        lax.fori_loop(0, tc, body, 0, unroll=2)

    def compute(prefetch_next):
        wait_gather(slot)
        wts = wts_ref[...]
        gate = [jnp.broadcast_to(wts[:, k:k + 1], (tc, LANES)) for k in range(TOP_K)]
        tokens_per_chunk = tc // packs
        for j in range(packs):
            if prefetch_next:
                for t in range(j * tokens_per_chunk, (j + 1) * tokens_per_chunk):
                    for k in range(TOP_K):
                        start_row(i + 1, 1 - slot, t, k)
            acc_lo = acc_hi = None
            for k in range(TOP_K):
                base = (slot * TOP_K + k) * (tc * pitch) + j
                lo, hi = _unpack_bf16_pairs(ybuf[pl.ds(base, tc, stride=pitch), :])
                acc_lo = gate[k] * lo if acc_lo is None else acc_lo + gate[k] * lo
                acc_hi = gate[k] * hi if acc_hi is None else acc_hi + gate[k] * hi
            c0 = j * PACK_COLS
            routed[:, c0:c0 + LANES] = acc_lo
            routed[:, c0 + LANES:c0 + PACK_COLS] = acc_hi
        y = alpha * h_ref[...] + (routed[...] + sh_ref[...])
        out = _layer_norm_rows(y, g_ref[...], b_ref[...])
        o_ref[...] = out
        obf_ref[...] = out.astype(BF16)

    @pl.when(i + 1 < n_steps)
    def _():
        compute(True)

    @pl.when(i + 1 == n_steps)
    def _():
        compute(False)


def _combine(dest_flat, y_sorted, h, shared, wts, g, b, layer, alpha, tc):
    m, d = h.shape
    tc = min(tc, m)
    row = pl.BlockSpec((tc, d), lambda i, dest: (i, 0))
    vec = pl.BlockSpec((None, 1, d), lambda i, dest: (layer, 0, 0))
    grid_spec = pltpu.PrefetchScalarGridSpec(
        num_scalar_prefetch=1,
        grid=(m // tc,),
        in_specs=[
            pl.BlockSpec(memory_space=pl.ANY),
            row, row,
            pl.BlockSpec((tc, IDX_LANES), lambda i, dest: (i, 0)),
            vec, vec,
        ],
        out_specs=[row, row],
        scratch_shapes=[pltpu.VMEM((2 * TOP_K * tc * _gather_pitch(d // PACK_COLS), LANES), I32),
                        pltpu.VMEM((tc, d), F32), pltpu.SemaphoreType.DMA((2,))],
    )
    return pl.pallas_call(
        functools.partial(_combine_kernel, tc=tc, alpha=alpha),
        grid_spec=grid_spec,
        out_shape=[jax.ShapeDtypeStruct((m, d), F32), jax.ShapeDtypeStruct((m, d), BF16)],
        compiler_params=_params(("arbitrary",)),
        name="moe_combine_ln",
    )(dest_flat, y_sorted, h, shared, wts, g, b)


def kernel(x, w_in, w_gate, b_gate, w_branch, w_out, gm_ln_g, gm_ln_b, w_spatial, b_spatial, conv_w,
           ln1_g, ln1_b, w_router, b_router, w_e_gate, w_e_up, w_e_down, w_s_gate, w_s_up, w_s_down,
           ln2_g, ln2_b):
    batch, seq, d = x.shape
    depth = w_in.shape[0]
    m = batch * seq
    bw = d // 4
    heads = bw // HEAD_DIM
    n_exp = w_router.shape[-1]
    alpha = float((2 * depth) ** 0.25)
    tile = EXPERT_ROW_TILE
    n_tiles = (m * TOP_K + n_exp * (tile - 1)) // tile + 1
    n_rows = n_tiles * tile

    b_gate4 = b_gate.reshape(depth, N_BRANCH, 1, d)
    gm_g3 = gm_ln_g.reshape(depth, 1, bw)
    gm_b3 = gm_ln_b.reshape(depth, 1, bw)
    b_sp_bcast = jnp.broadcast_to(b_spatial[..., None], b_spatial.shape + (GM_SPAN,))
    ln1_g3, ln1_b3 = ln1_g.reshape(depth, 1, d), ln1_b.reshape(depth, 1, d)
    ln2_g3, ln2_b3 = ln2_g.reshape(depth, 1, d), ln2_b.reshape(depth, 1, d)
    b_router3 = b_router.reshape(depth, 1, n_exp)

    h = x.reshape(m, d)
    h_bf = h.astype(BF16)
    for l in range(depth):
        qkv = _matmul(h_bf, w_in, l, 0, 3 * bw, *MATMUL_TILE, BF16, "proj_qkv")
        rest = _matmul(h_bf, w_in, l, 3 * bw, 5 * bw, *MATMUL_TILE, F32, "proj_rest")
        o_a = _attention(qkv, batch, seq, heads, *ATTN_TILE)
        o_bc = _gmlp_conv(rest, gm_g3, gm_b3, w_spatial, b_sp_bcast, conv_w, l, seq, bw, TOKEN_TILE)
        merged = _merge(h_bf, o_a, o_bc, w_gate, b_gate4, w_branch, l, *MERGE_TILE)
        mix = _matmul(merged, w_out, l, 0, d, *MATMUL_TILE, F32, "out_proj")
        h, h_bf, h_pk = _residual_ln(h, mix, ln1_g3, ln1_b3, l, alpha, TOKEN_TILE)

        idx, rank, wts, counts = _router(h, w_router, b_router3, l, ROUTER_TILE)
        dest2d, tile_expert, n_used, next_used = _route_plan(counts, idx, rank, tile, n_tiles)
        dest = dest2d.reshape(-1)
        src = _inverse_map(dest, n_rows)
        y_sorted = _experts(tile_expert, n_used, src, next_used, h_pk, w_e_gate, w_e_up, w_e_down, l, n_tiles, tile)
        act_s = _swiglu(h_bf, w_s_gate, w_s_up, l, *SWIGLU_TILE)
        shared = _matmul(act_s, w_s_down, l, 0, d, *MATMUL_TILE, F32, "shared_down")
        h, h_bf = _combine(dest, y_sorted, h, shared, wts, ln2_g3, ln2_b3, l, alpha, COMBINE_TILE)
    return h.reshape(batch, seq, d)
```

```python
import functools

import jax
import jax.numpy as jnp
from jax import lax
from jax.experimental import pallas as pl
from jax.experimental.pallas import tpu as pltpu

F32 = jnp.float32
BF16 = jnp.bfloat16
I32 = jnp.int32

CHUNK = 64
HEAD_DIM = 128
GM_SPAN = 128
GM_GROUP_DIM = 128
IN_PARTS = 8
N_BRANCH = 3
TOP_K = 6
ROUTED_SCALE = 2.5
LN_EPS = 1e-5
LOG2_E = 1.4426950408889634

V7X_VMEM_LIMIT_BYTES = 56 * 1024 * 1024
LANES = 128
SUBLANES = 8
CAST_ROWS = 256
EXPERT_ROW_TILE = 256
GATHER_SLOTS = 3
IDX_LANES = 8
MATMUL_TILE = (1024, 512)
SHARED_DOWN_TILE = (2048, 1024)
SWIGLU_TILE = (1024, 256)
MERGE_TILE = (512, 256)
ATTN_TILE = (512, 512, 2)
TOKEN_TILE = 256
ROUTER_TILE = 512
COMBINE_TILE = 128


def _params(sem):
    return pltpu.CompilerParams(dimension_semantics=sem, vmem_limit_bytes=V7X_VMEM_LIMIT_BYTES)


def _cast_rows_to_bf16(src_ref, dst_ref):
    k = src_ref.shape[0]
    rows = min(CAST_ROWS, k)

    def body(i, c):
        r = pl.multiple_of(i * rows, rows)
        dst_ref[pl.ds(r, rows), :] = src_ref[pl.ds(r, rows), :].astype(BF16)
        return c

    lax.fori_loop(0, k // rows, body, 0)


def _dot(a, b):
    return jnp.dot(a, b, preferred_element_type=F32)


PACK_COLS = 2 * LANES
HIGH_HALF = -65536


def _pack_bf16_pairs(lo, hi):
    lo_bits = lax.bitcast_convert_type(lo.astype(BF16).astype(F32), I32)
    hi_bits = lax.bitcast_convert_type(hi.astype(BF16).astype(F32), I32)
    return hi_bits | lax.shift_right_logical(lo_bits, 16)


def _unpack_bf16_pairs(word):
    lo = lax.bitcast_convert_type(word << 16, F32)
    hi = lax.bitcast_convert_type(word & HIGH_HALF, F32)
    return lo, hi


def _gather_pitch(packs):
    pitch = -(-packs // SUBLANES) * SUBLANES
    return pitch + SUBLANES if pitch % (2 * SUBLANES) == 0 else pitch


def _store_packed_rows(values, pk_ref):
    rows, cols = values.shape
    packs = cols // PACK_COLS
    for j in range(packs):
        c0 = j * PACK_COLS
        word = _pack_bf16_pairs(values[:, c0:c0 + LANES], values[:, c0 + LANES:c0 + PACK_COLS])
        pk_ref[pl.ds(j, rows, stride=packs), :] = word


def _mm_kernel(a_ref, w_ref, o_ref, wbf_ref):
    @pl.when(pl.program_id(1) == 0)
    def _():
        _cast_rows_to_bf16(w_ref, wbf_ref)

    o_ref[...] = _dot(a_ref[...], wbf_ref[...]).astype(o_ref.dtype)


def _matmul(a, w, layer, col_off, n_cols, tm, tn, out_dtype, name):
    m, k = a.shape
    tm = min(tm, m)
    while n_cols % tn or col_off % tn:
        tn //= 2
    col_block_off = col_off // tn
    return pl.pallas_call(
        _mm_kernel,
        grid=(n_cols // tn, m // tm),
        in_specs=[
            pl.BlockSpec((tm, k), lambda n, i: (i, 0)),
            pl.BlockSpec((None, k, tn), lambda n, i: (layer, 0, n + col_block_off)),
        ],
        out_specs=pl.BlockSpec((tm, tn), lambda n, i: (i, n)),
        out_shape=jax.ShapeDtypeStruct((m, n_cols), out_dtype),
        scratch_shapes=[pltpu.VMEM((k, tn), BF16)],
        compiler_params=_params(("arbitrary", "arbitrary")),
        name=name,
    )(a, w)


def _attn_kernel(q_ref, k_ref, v_ref, o_ref, *, tq, kb, ks, hp, scale):
    qi = pl.program_id(2)
    n_sub = ks // kb
    n_steps = ((qi + 1) * tq + ks - 1) // ks
    t_pos = lax.broadcasted_iota(I32, (tq, kb), 0) + qi * tq
    s_loc = lax.broadcasted_iota(I32, (tq, kb), 1)
    r = lax.broadcasted_iota(I32, (2 * kb, 2 * kb), 0)
    c = lax.broadcasted_iota(I32, (2 * kb, 2 * kb), 1)
    r = jnp.where(r >= kb, r - kb, r)
    suffix = jnp.where((c >= kb) | (r > c), 1.0, 0.0).astype(BF16)

    def step(j, carry, masked):
        k0 = pl.multiple_of(j * ks, ks)
        return tuple(head_step(h, k0, carry[h], masked) for h in range(hp))

    def head_step(h, k0, carry, masked):
        acc, later_sum = carry
        cols = slice(h * HEAD_DIM, (h + 1) * HEAD_DIM)
        q = q_ref[:, cols]
        kj = k_ref[pl.ds(k0, ks), cols]
        vj = v_ref[pl.ds(k0, ks), cols]
        z = lax.dot_general(q, kj, (((1,), (1,)), ((), ())), preferred_element_type=F32) * (scale * LOG2_E)
        weights = [None] * n_sub
        for s in reversed(range(n_sub)):
            zs = z[:, s * kb:(s + 1) * kb]
            softplus = jnp.log(1.0 + jnp.exp2(jnp.minimum(zs, -zs))) * LOG2_E
            log_beta = jnp.minimum(zs, 0.0) - softplus
            log_one_minus = log_beta - zs
            if masked:
                strict = (s_loc + (k0 + s * kb)) < t_pos
                log_one_minus = jnp.where(strict, log_one_minus, 0.0)
            hi = log_one_minus.astype(BF16)
            lo = (log_one_minus - hi.astype(F32)).astype(BF16)
            sums = _dot(jnp.concatenate([hi, lo], axis=1), suffix)
            a = jnp.exp2(log_beta + (sums[:, :kb] + later_sum))
            if masked:
                a = jnp.where(strict, a, 0.0)
            weights[s] = a.astype(BF16)
            later_sum = later_sum + sums[:, kb:]
        acc = acc + _dot(jnp.concatenate(weights, axis=1), vj)
        return acc, later_sum

    carry = tuple((jnp.zeros((tq, HEAD_DIM), F32), jnp.zeros((tq, kb), F32)) for _ in range(hp))
    carry = step(n_steps - 1, carry, True)
    carry = lax.fori_loop(0, n_steps - 1, lambda t, cr: step(n_steps - 2 - t, cr, False), carry)
    for h in range(hp):
        o_ref[:, h * HEAD_DIM:(h + 1) * HEAD_DIM] = carry[h][0].astype(o_ref.dtype)


def _attention(qkv, batch, seq, heads, tq, ks, hp):
    m = qkv.shape[0]
    tq = min(tq, seq)
    ks = min(ks, seq)
    hp = min(hp, heads)
    assert ks % tq == 0 and heads % hp == 0
    kb = HEAD_DIM
    nq = seq // tq
    groups = heads // hp
    width = hp * HEAD_DIM
    kern = functools.partial(_attn_kernel, tq=tq, kb=kb, ks=ks, hp=hp, scale=HEAD_DIM ** -0.5)
    return pl.pallas_call(
        kern,
        grid=(batch, groups, nq),
        in_specs=[
            pl.BlockSpec((tq, width), lambda b, g, i: (b * nq + i, g)),
            pl.BlockSpec((seq, width), lambda b, g, i: (b, groups + g)),
            pl.BlockSpec((seq, width), lambda b, g, i: (b, 2 * groups + g)),
        ],
        out_specs=pl.BlockSpec((tq, width), lambda b, g, i: (b * nq + i, g)),
        out_shape=jax.ShapeDtypeStruct((m, heads * HEAD_DIM), BF16),
        compiler_params=_params(("arbitrary", "arbitrary", "arbitrary")),
        name="sb_attention",
    )(qkv, qkv, qkv)


def _gelu_tanh(x):
    return 0.5 * x * (1.0 + jnp.tanh(0.7978845608028654 * (x + 0.044715 * (x * x * x))))


def _gmlp_conv_kernel(gu_ref, gv_ref, cb_ref, cc_ref, cx_ref, ccp_ref, cxp_ref, lng_ref, lnb_ref,
                      wsp_ref, bsp_ref, cw_ref, o_ref, *, tt, seq, groups):
    i = pl.program_id(0)
    z = cc_ref[...] * cx_ref[...]
    first = (i * tt) % seq == 0
    zp = jnp.where(first, 0.0, ccp_ref[...] * cxp_ref[...])
    row = lax.broadcasted_iota(I32, z.shape, 0)
    z1 = jnp.where(row == 0, zp[SUBLANES - 1:SUBLANES, :], pltpu.roll(z, 1, 0))
    z2 = pltpu.roll(z, 2, 0)
    z2 = jnp.where(row == 0, zp[SUBLANES - 2:SUBLANES - 1, :], z2)
    z2 = jnp.where(row == 1, zp[SUBLANES - 1:SUBLANES, :], z2)
    y = cw_ref[0:1, :] * z2 + cw_ref[1:2, :] * z1 + cw_ref[2:3, :] * z
    o_ref[1] = (cb_ref[...] * y).astype(o_ref.dtype)

    v = _gelu_tanh(gv_ref[...])
    mu = jnp.mean(v, axis=-1, keepdims=True)
    var = jnp.mean(jnp.square(v - mu), axis=-1, keepdims=True)
    vn = ((v - mu) * lax.rsqrt(var + LN_EPS) * lng_ref[...] + lnb_ref[...]).astype(BF16)
    u = _gelu_tanh(gu_ref[...])
    tr = lax.broadcasted_iota(I32, (GM_SPAN, GM_SPAN), 0) // CHUNK
    sc = lax.broadcasted_iota(I32, (GM_SPAN, GM_SPAN), 1) // CHUNK
    causal = tr >= sc
    for g in range(groups):
        wm = jnp.where(causal, wsp_ref[g], 0.0).astype(BF16)
        cols = slice(g * GM_GROUP_DIM, (g + 1) * GM_GROUP_DIM)
        for s in range(tt // GM_SPAN):
            rows = slice(s * GM_SPAN, (s + 1) * GM_SPAN)
            f = _dot(wm, vn[rows, cols]) + bsp_ref[g]
            o_ref[0, rows, cols] = (u[rows, cols] * f).astype(o_ref.dtype)


def _gmlp_conv(rest, gm_ln_g, gm_ln_b, w_spatial, b_spatial_bcast, conv_w, layer, seq, bw, tt):
    m = rest.shape[0]
    tt = min(tt, seq)
    groups = bw // GM_GROUP_DIM
    halo_blocks = tt // SUBLANES
    kern = functools.partial(_gmlp_conv_kernel, tt=tt, seq=seq, groups=groups)

    def part(p):
        return pl.BlockSpec((tt, bw), lambda i: (i, p))

    def halo(p):
        return pl.BlockSpec((SUBLANES, bw), lambda i: (jnp.maximum(i * halo_blocks - 1, 0), p))

    return pl.pallas_call(
        kern,
        grid=(m // tt,),
        in_specs=[
            part(0), part(1), part(2), part(3), part(4), halo(3), halo(4),
            pl.BlockSpec((None, 1, bw), lambda i: (layer, 0, 0)),
            pl.BlockSpec((None, 1, bw), lambda i: (layer, 0, 0)),
            pl.BlockSpec((None, groups, GM_SPAN, GM_SPAN), lambda i: (layer, 0, 0, 0)),
            pl.BlockSpec((None, groups, GM_SPAN, GM_SPAN), lambda i: (layer, 0, 0, 0)),
            pl.BlockSpec((None, 3, bw), lambda i: (layer, 0, 0)),
        ],
        out_specs=pl.BlockSpec((2, tt, bw), lambda i: (0, i, 0)),
        out_shape=jax.ShapeDtypeStruct((2, m, bw), BF16),
        compiler_params=_params(("arbitrary",)),
        name="gmlp_conv",
    )(rest, rest, rest, rest, rest, rest, rest, gm_ln_g, gm_ln_b, w_spatial, b_spatial_bcast, conv_w)


def _merge_kernel(h_ref, oa_ref, obc_ref, wg_ref, bg_ref, wb_ref, o_ref, wg_bf, wb_bf):
    @pl.when(pl.program_id(1) == 0)
    def _():
        for i in range(N_BRANCH):
            _cast_rows_to_bf16(wg_ref.at[i], wg_bf.at[i])
            _cast_rows_to_bf16(wb_ref.at[i], wb_bf.at[i])

    h = h_ref[...]
    merged = None
    for i in range(N_BRANCH):
        gate = jax.nn.sigmoid(_dot(h, wg_bf[i]) + bg_ref[i])
        branch = oa_ref[...] if i == 0 else obc_ref[i - 1]
        term = gate * _dot(branch, wb_bf[i])
        merged = term if merged is None else merged + term
    o_ref[...] = merged.astype(o_ref.dtype)


def _merge(h_bf, o_a, o_bc, w_gate, b_gate4, w_branch, layer, tm, tn):
    m, d = h_bf.shape
    bw = o_a.shape[1]
    tm = min(tm, m)
    tn = min(tn, d)
    return pl.pallas_call(
        _merge_kernel,
        grid=(d // tn, m // tm),
        in_specs=[
            pl.BlockSpec((tm, d), lambda n, i: (i, 0)),
            pl.BlockSpec((tm, bw), lambda n, i: (i, 0)),
            pl.BlockSpec((2, tm, bw), lambda n, i: (0, i, 0)),
            pl.BlockSpec((None, N_BRANCH, d, tn), lambda n, i: (layer, 0, 0, n)),
            pl.BlockSpec((None, N_BRANCH, 1, tn), lambda n, i: (layer, 0, 0, n)),
            pl.BlockSpec((None, N_BRANCH, bw, tn), lambda n, i: (layer, 0, 0, n),
                         pipeline_mode=pl.Buffered(1)),
        ],
        out_specs=pl.BlockSpec((tm, tn), lambda n, i: (i, n)),
        out_shape=jax.ShapeDtypeStruct((m, d), BF16),
        scratch_shapes=[pltpu.VMEM((N_BRANCH, d, tn), BF16), pltpu.VMEM((N_BRANCH, bw, tn), BF16)],
        compiler_params=_params(("arbitrary", "arbitrary")),
        name="branch_merge",
    )(h_bf, o_a, o_bc, w_gate, b_gate4, w_branch)


def _layer_norm_rows(y, g, b):
    mu = jnp.mean(y, axis=-1, keepdims=True)
    var = jnp.mean(jnp.square(y - mu), axis=-1, keepdims=True)
    return (y - mu) * lax.rsqrt(var + LN_EPS) * g + b


def _residual_ln_kernel(h_ref, f_ref, g_ref, b_ref, o_ref, obf_ref, opk_ref, *, alpha):
    out = _layer_norm_rows(alpha * h_ref[...] + f_ref[...], g_ref[...], b_ref[...])
    o_ref[...] = out
    obf_ref[...] = out.astype(BF16)
    _store_packed_rows(out, opk_ref)


def _residual_ln(h, f, g, b, layer, alpha, tm):
    m, d = h.shape
    tm = min(tm, m)
    row = pl.BlockSpec((tm, d), lambda i: (i, 0))
    vec = pl.BlockSpec((None, 1, d), lambda i: (layer, 0, 0))
    packs = d // PACK_COLS
    packed = pl.BlockSpec((tm * packs, LANES), lambda i: (i, 0))
    return pl.pallas_call(
        functools.partial(_residual_ln_kernel, alpha=alpha),
        grid=(m // tm,),
        in_specs=[row, row, vec, vec],
        out_specs=[row, row, packed],
        out_shape=[jax.ShapeDtypeStruct((m, d), F32), jax.ShapeDtypeStruct((m, d), BF16),
                   jax.ShapeDtypeStruct((m * packs, LANES), I32)],
        compiler_params=_params(("arbitrary",)),
        name="residual_ln",
    )(h, f, g, b)


def _router_kernel(h_ref, w_ref, b_ref, idx_ref, rank_ref, wts_ref, cnt_ref, run_ref, *, tr, n_exp):
    @pl.when(pl.program_id(0) == 0)
    def _():
        run_ref[...] = jnp.zeros_like(run_ref)

    h = h_ref[...]
    w = w_ref[...]
    h_hi = h.astype(BF16)
    h_lo = (h - h_hi.astype(F32)).astype(BF16)
    w_hi = w.astype(BF16)
    w_lo = (w - w_hi.astype(F32)).astype(BF16)
    logits = _dot(h_hi, w_hi) + (_dot(h_hi, w_lo) + _dot(h_lo, w_hi))
    scores = jax.nn.sigmoid(logits)
    sel = scores + b_ref[...]
    lane = lax.broadcasted_iota(I32, (tr, n_exp), 1).astype(F32)
    out_lane = lax.broadcasted_iota(I32, (tr, IDX_LANES), 1)
    mask = jnp.zeros((tr, n_exp), F32)
    idx_out = jnp.zeros((tr, IDX_LANES), F32)
    picks = []
    for k in range(TOP_K):
        best = jnp.max(sel, axis=-1, keepdims=True)
        pick = jnp.min(jnp.where(sel == best, lane, float(n_exp)), axis=-1, keepdims=True)
        hit = lane == pick
        mask = jnp.where(hit, 1.0, mask)
        sel = jnp.where(hit, -jnp.inf, sel)
        idx_out = jnp.where(out_lane == k, pick, idx_out)
        picks.append(hit)
    chosen_scores = mask * scores
    gates = chosen_scores / jnp.sum(chosen_scores, axis=-1, keepdims=True) * ROUTED_SCALE

    rr = lax.broadcasted_iota(I32, (tr, tr), 0)
    cc = lax.broadcasted_iota(I32, (tr, tr), 1)
    lower = jnp.where(rr > cc, 1.0, 0.0).astype(BF16)
    rank = _dot(lower, mask.astype(BF16)) + run_ref[...]
    run_ref[...] = run_ref[...] + jnp.sum(mask, axis=0, keepdims=True)
    cnt_ref[...] = run_ref[...].astype(I32)

    rank_out = jnp.zeros((tr, IDX_LANES), F32)
    wts_out = jnp.zeros((tr, IDX_LANES), F32)
    for k in range(TOP_K):
        rank_out = jnp.where(out_lane == k, jnp.sum(jnp.where(picks[k], rank, 0.0), axis=-1, keepdims=True), rank_out)
        wts_out = jnp.where(out_lane == k, jnp.sum(jnp.where(picks[k], gates, 0.0), axis=-1, keepdims=True), wts_out)
    idx_ref[...] = idx_out.astype(I32)
    rank_ref[...] = rank_out.astype(I32)
    wts_ref[...] = wts_out


def _router(h, w_router, b_router3, layer, tr):
    m, d = h.shape
    n_exp = w_router.shape[-1]
    tr = min(tr, m)
    lists = pl.BlockSpec((tr, IDX_LANES), lambda i: (i, 0))
    return pl.pallas_call(
        functools.partial(_router_kernel, tr=tr, n_exp=n_exp),
        grid=(m // tr,),
        in_specs=[
            pl.BlockSpec((tr, d), lambda i: (i, 0)),
            pl.BlockSpec((None, d, n_exp), lambda i: (layer, 0, 0)),
            pl.BlockSpec((None, 1, n_exp), lambda i: (layer, 0, 0)),
        ],
        out_specs=[lists, lists, lists, pl.BlockSpec((1, n_exp), lambda i: (0, 0))],
        out_shape=[
            jax.ShapeDtypeStruct((m, IDX_LANES), I32),
            jax.ShapeDtypeStruct((m, IDX_LANES), I32),
            jax.ShapeDtypeStruct((m, IDX_LANES), F32),
            jax.ShapeDtypeStruct((1, n_exp), I32),
        ],
        scratch_shapes=[pltpu.VMEM((1, n_exp), F32)],
        compiler_params=_params(("arbitrary",)),
        name="router_topk",
    )(h, w_router, b_router3)


def _inverse_kernel(dest_ref, zeros_hbm, src_ref, sem, *, n_pairs):
    clear = pltpu.make_async_copy(zeros_hbm, src_ref, sem)
    clear.start()
    clear.wait()

    def place(t, c):
        for k in range(TOP_K):
            src_ref[dest_ref[t * TOP_K + k]] = t
        return c

    lax.fori_loop(0, n_pairs // TOP_K, place, 0, unroll=2)


def _inverse_map(dest_flat, n_rows):
    n_pairs = dest_flat.shape[0]
    smem = pl.BlockSpec(memory_space=pltpu.SMEM)
    return pl.pallas_call(
        functools.partial(_inverse_kernel, n_pairs=n_pairs),
        in_specs=[smem, pl.BlockSpec(memory_space=pl.ANY)],
        out_specs=smem,
        out_shape=jax.ShapeDtypeStruct((n_rows,), I32),
        scratch_shapes=[pltpu.SemaphoreType.DMA(())],
        name="inverse_map",
    )(dest_flat, jnp.zeros((n_rows,), I32))


def _route_plan_kernel(cnt_ref, idx_ref, rank_ref, dest_ref, te_ref, nu_ref, nxt_ref, off_ref, *,
                       n_exp, tile, n_tiles):
    shift = tile.bit_length() - 1

    def link(t, following):
        e = n_exp - 1 - t
        nxt_ref[e] = following
        return jnp.where(cnt_ref[e] > 0, e, following)

    lax.fori_loop(0, n_exp, link, jnp.int32(-1))

    def per_expert(e, carry):
        tiles_before, last_owner = carry
        nt = lax.shift_right_logical(cnt_ref[e] + (tile - 1), shift)
        off_ref[e] = tiles_before * tile

        def fill(q, c):
            te_ref[tiles_before + q] = e
            return c

        lax.fori_loop(0, nt, fill, 0)
        return tiles_before + nt, jnp.where(nt > 0, e, last_owner)

    used, last_owner = lax.fori_loop(0, n_exp, per_expert, (jnp.int32(0), jnp.int32(0)))
    nu_ref[0] = used

    def fill_tail(q, c):
        te_ref[q] = last_owner
        return c

    lax.fori_loop(used, n_tiles, fill_tail, 0)

    idx = idx_ref[...]
    offset = lax.fori_loop(0, n_exp, lambda e, acc: jnp.where(idx == e, off_ref[e], acc),
                           jnp.zeros_like(idx))
    dest_ref[...] = offset + rank_ref[...]


def _route_plan(counts, idx, rank, tile, n_tiles):
    n_exp = counts.shape[-1]
    pairs = idx[:, :TOP_K].reshape(-1, LANES)
    ranks = rank[:, :TOP_K].reshape(-1, LANES)
    smem = pl.BlockSpec(memory_space=pltpu.SMEM)
    vmem = pl.BlockSpec(memory_space=pltpu.VMEM)
    assert tile & (tile - 1) == 0
    return pl.pallas_call(
        functools.partial(_route_plan_kernel, n_exp=n_exp, tile=tile, n_tiles=n_tiles),
        in_specs=[smem, vmem, vmem],
        out_specs=[vmem, smem, smem, smem],
        out_shape=[jax.ShapeDtypeStruct(pairs.shape, I32), jax.ShapeDtypeStruct((n_tiles,), I32),
                   jax.ShapeDtypeStruct((1,), I32), jax.ShapeDtypeStruct((n_exp,), I32)],
        scratch_shapes=[pltpu.SMEM((n_exp,), I32)],
        name="route_plan",
    )(counts.reshape(n_exp), pairs, ranks)


def _expert_kernel(te_ref, nu_ref, src_ref, nxt_ref, h_hbm, wg_hbm, wu_hbm, wd_hbm, o_ref,
                   xbuf, x_bf, wg_f32, wu_f32, wd_f32, wg_bf, wu_bf, wd_bf, wslot, sem, wsem,
                   *, tile, layer):
    i = pl.program_id(0)
    n_used = nu_ref[0]
    slot = lax.rem(i, GATHER_SLOTS)

    packs = x_bf.shape[1] // PACK_COLS
    pitch = xbuf.shape[0] // (GATHER_SLOTS * tile)

    def weight_copies(expert, s):
        return [pltpu.make_async_copy(hbm.at[layer, expert], buf.at[s], wsem.at[s, n])
                for n, (hbm, buf) in enumerate(((wg_hbm, wg_f32), (wu_hbm, wu_f32), (wd_hbm, wd_f32)))]

    def start_row(tile_idx, s, r):
        src_row = pl.multiple_of(src_ref[tile_idx * tile + r] * packs, packs)
        dst_row = pl.multiple_of((s * tile + r) * pitch, SUBLANES)
        pltpu.make_async_copy(h_hbm.at[pl.ds(src_row, packs)], xbuf.at[pl.ds(dst_row, packs)],
                              sem.at[s]).start()

    def wait_gather(s):
        dst_row = pl.multiple_of(s * (tile * pitch), SUBLANES)
        pltpu.make_async_copy(h_hbm.at[pl.ds(0, tile * packs)], xbuf.at[pl.ds(dst_row, tile * packs)],
                              sem.at[s]).wait()

    @pl.when(i == 0)
    def _():
        wslot[0] = 0
        for cp in weight_copies(te_ref[0], 0):
            cp.start(priority=1)

        for first in range(GATHER_SLOTS - 1):
            @pl.when(first < n_used)
            def _():
                def body(r, c):
                    start_row(first, first, r)
                    return c
                lax.fori_loop(0, tile, body, 0, unroll=8)

    def compute(prefetch):
        expert = te_ref[i]
        changed = jnp.logical_or(i == 0, expert != te_ref[jnp.maximum(i - 1, 0)])

        @pl.when(changed)
        def _():
            s = wslot[0]
            for cp in weight_copies(expert, s):
                cp.wait()
            following = nxt_ref[expert]

            @pl.when(following >= 0)
            def _():
                for cp in weight_copies(following, 1 - s):
                    cp.start(priority=1)

            _cast_rows_to_bf16(wg_f32.at[s], wg_bf)
            _cast_rows_to_bf16(wu_f32.at[s], wu_bf)
            _cast_rows_to_bf16(wd_f32.at[s], wd_bf)
            wslot[0] = 1 - s

        wait_gather(slot)
        rows_per_chunk = tile // packs
        ahead = GATHER_SLOTS - 1
        ahead_slot = lax.rem(i + ahead, GATHER_SLOTS)
        for j in range(packs):
            if prefetch:
                for r in range(j * rows_per_chunk, (j + 1) * rows_per_chunk):
                    start_row(i + ahead, ahead_slot, r)
            lo, hi = _unpack_bf16_pairs(xbuf[pl.ds(slot * (tile * pitch) + j, tile, stride=pitch), :])
            c0 = j * PACK_COLS
            x_bf[:, c0:c0 + LANES] = lo.astype(BF16)
            x_bf[:, c0 + LANES:c0 + PACK_COLS] = hi.astype(BF16)
        x = x_bf[...]
        act = jax.nn.silu(_dot(x, wg_bf[...])) * _dot(x, wu_bf[...])
        _store_packed_rows(_dot(act.astype(BF16), wd_bf[...]), o_ref)

    @pl.when(i + (GATHER_SLOTS - 1) < n_used)
    def _():
        compute(True)

    @pl.when(jnp.logical_and(i < n_used, i + (GATHER_SLOTS - 1) >= n_used))
    def _():
        compute(False)

    @pl.when(i >= n_used)
    def _():
        o_ref[...] = jnp.zeros_like(o_ref)


def _experts(tile_expert, n_used, src, next_used, h_pk, w_e_gate, w_e_up, w_e_down, layer, n_tiles, tile):
    d, ff = w_e_gate.shape[-2:]
    packs = d // PACK_COLS
    hbm = pl.BlockSpec(memory_space=pl.ANY)

    grid_spec = pltpu.PrefetchScalarGridSpec(
        num_scalar_prefetch=4,
        grid=(n_tiles,),
        in_specs=[hbm, hbm, hbm, hbm],
        out_specs=pl.BlockSpec((tile * packs, LANES), lambda i, te, nu, s, nx: (i, 0)),
        scratch_shapes=[
            pltpu.VMEM((GATHER_SLOTS * tile * _gather_pitch(packs), LANES), I32),
            pltpu.VMEM((tile, d), BF16),
            pltpu.VMEM((2, d, ff), F32),
            pltpu.VMEM((2, d, ff), F32),
            pltpu.VMEM((2, ff, d), F32),
            pltpu.VMEM((d, ff), BF16),
            pltpu.VMEM((d, ff), BF16),
            pltpu.VMEM((ff, d), BF16),
            pltpu.SMEM((1,), I32),
            pltpu.SemaphoreType.DMA((GATHER_SLOTS,)),
            pltpu.SemaphoreType.DMA((2, 3)),
        ],
    )
    return pl.pallas_call(
        functools.partial(_expert_kernel, tile=tile, layer=layer),
        grid_spec=grid_spec,
        out_shape=jax.ShapeDtypeStruct((n_tiles * tile * packs, LANES), I32),
        compiler_params=_params(("arbitrary",)),
        name="expert_mlp",
    )(tile_expert, n_used, src, next_used, h_pk, w_e_gate, w_e_up, w_e_down)


def _swiglu_kernel(a_ref, wg_ref, wu_ref, o_ref, wg_bf, wu_bf):
    @pl.when(pl.program_id(1) == 0)
    def _():
        _cast_rows_to_bf16(wg_ref, wg_bf)
        _cast_rows_to_bf16(wu_ref, wu_bf)

    a = a_ref[...]
    o_ref[...] = (jax.nn.silu(_dot(a, wg_bf[...])) * _dot(a, wu_bf[...])).astype(o_ref.dtype)


def _swiglu(a, w_gate, w_up, layer, tm, tn):
    m, k = a.shape
    n = w_gate.shape[-1]
    tm = min(tm, m)
    tn = min(tn, n)
    wspec = pl.BlockSpec((None, k, tn), lambda j, i: (layer, 0, j))
    return pl.pallas_call(
        _swiglu_kernel,
        grid=(n // tn, m // tm),
        in_specs=[pl.BlockSpec((tm, k), lambda j, i: (i, 0)), wspec, wspec],
        out_specs=pl.BlockSpec((tm, tn), lambda j, i: (i, j)),
        out_shape=jax.ShapeDtypeStruct((m, n), BF16),
        scratch_shapes=[pltpu.VMEM((k, tn), BF16), pltpu.VMEM((k, tn), BF16)],
        compiler_params=_params(("arbitrary", "arbitrary")),
        name="shared_swiglu",
    )(a, w_gate, w_up)


def _combine_kernel(dest_ref, y_hbm, h_ref, sh_ref, wts_ref, g_ref, b_ref, o_ref, obf_ref,
                    ybuf, routed, sem, *, tc, alpha):
    i = pl.program_id(0)
    n_steps = pl.num_programs(0)
    slot = i % 2

    packs = routed.shape[1] // PACK_COLS
    pitch = ybuf.shape[0] // (2 * TOP_K * tc)

    def start_row(step, s, t, k):
        src_row = pl.multiple_of(dest_ref[(step * tc + t) * TOP_K + k] * packs, packs)
        dst_row = pl.multiple_of(((s * TOP_K + k) * tc + t) * pitch, SUBLANES)
        pltpu.make_async_copy(y_hbm.at[pl.ds(src_row, packs)], ybuf.at[pl.ds(dst_row, packs)],
                              sem.at[s]).start(priority=k % 2)

    def wait_gather(s):
        dst_row = pl.multiple_of(s * (TOP_K * tc * pitch), SUBLANES)
        pltpu.make_async_copy(y_hbm.at[pl.ds(0, TOP_K * tc * packs)],
                              ybuf.at[pl.ds(dst_row, TOP_K * tc * packs)], sem.at[s]).wait()

    @pl.when(i == 0)
    def _():
        def body(t, c):
            for k in range(TOP_K):
                start_row(0, 0, t, k)
            return c
        lax.fori_loop(0, tc, body, 0, unroll=2)

    def compute(prefetch_next):
        wait_gather(slot)
        wts = wts_ref[...]
        gate = [jnp.broadcast_to(wts[:, k:k + 1], (tc, LANES)) for k in range(TOP_K)]
        tokens_per_chunk = tc // packs
        for j in range(packs):
            if prefetch_next:
                for t in range(j * tokens_per_chunk, (j + 1) * tokens_per_chunk):
                    for k in range(TOP_K):
                        start_row(i + 1, 1 - slot, t, k)
            acc_lo = acc_hi = None
            for k in range(TOP_K):
                base = (slot * TOP_K + k) * (tc * pitch) + j
                lo, hi = _unpack_bf16_pairs(ybuf[pl.ds(base, tc, stride=pitch), :])
                acc_lo = gate[k] * lo if acc_lo is None else acc_lo + gate[k] * lo
                acc_hi = gate[k] * hi if acc_hi is None else acc_hi + gate[k] * hi
            c0 = j * PACK_COLS
            routed[:, c0:c0 + LANES] = acc_lo
            routed[:, c0 + LANES:c0 + PACK_COLS] = acc_hi
        y = alpha * h_ref[...] + (routed[...] + sh_ref[...])
        out = _layer_norm_rows(y, g_ref[...], b_ref[...])
        o_ref[...] = out
        obf_ref[...] = out.astype(BF16)

    @pl.when(i + 1 < n_steps)
    def _():
        compute(True)

    @pl.when(i + 1 == n_steps)
    def _():
        compute(False)


def _combine(dest_flat, y_sorted, h, shared, wts, g, b, layer, alpha, tc):
    m, d = h.shape
    tc = min(tc, m)
    row = pl.BlockSpec((tc, d), lambda i, dest: (i, 0))
    vec = pl.BlockSpec((None, 1, d), lambda i, dest: (layer, 0, 0))
    grid_spec = pltpu.PrefetchScalarGridSpec(
        num_scalar_prefetch=1,
        grid=(m // tc,),
        in_specs=[
            pl.BlockSpec(memory_space=pl.ANY),
            row, row,
            pl.BlockSpec((tc, IDX_LANES), lambda i, dest: (i, 0)),
            vec, vec,
        ],
        out_specs=[row, row],
        scratch_shapes=[pltpu.VMEM((2 * TOP_K * tc * _gather_pitch(d // PACK_COLS), LANES), I32),
                        pltpu.VMEM((tc, d), F32), pltpu.SemaphoreType.DMA((2,))],
    )
    return pl.pallas_call(
        functools.partial(_combine_kernel, tc=tc, alpha=alpha),
        grid_spec=grid_spec,
        out_shape=[jax.ShapeDtypeStruct((m, d), F32), jax.ShapeDtypeStruct((m, d), BF16)],
        compiler_params=_params(("arbitrary",)),
        name="moe_combine_ln",
    )(dest_flat, y_sorted, h, shared, wts, g, b)


def kernel(x, w_in, w_gate, b_gate, w_branch, w_out, gm_ln_g, gm_ln_b, w_spatial, b_spatial, conv_w,
           ln1_g, ln1_b, w_router, b_router, w_e_gate, w_e_up, w_e_down, w_s_gate, w_s_up, w_s_down,
           ln2_g, ln2_b):
    batch, seq, d = x.shape
    depth = w_in.shape[0]
    m = batch * seq
    bw = d // 4
    heads = bw // HEAD_DIM
    n_exp = w_router.shape[-1]
    alpha = float((2 * depth) ** 0.25)
    tile = EXPERT_ROW_TILE
    n_tiles = (m * TOP_K + n_exp * (tile - 1)) // tile + 1
    n_rows = n_tiles * tile

    b_gate4 = b_gate.reshape(depth, N_BRANCH, 1, d)
    gm_g3 = gm_ln_g.reshape(depth, 1, bw)
    gm_b3 = gm_ln_b.reshape(depth, 1, bw)
    b_sp_bcast = jnp.broadcast_to(b_spatial[..., None], b_spatial.shape + (GM_SPAN,))
    ln1_g3, ln1_b3 = ln1_g.reshape(depth, 1, d), ln1_b.reshape(depth, 1, d)
    ln2_g3, ln2_b3 = ln2_g.reshape(depth, 1, d), ln2_b.reshape(depth, 1, d)
    b_router3 = b_router.reshape(depth, 1, n_exp)

    h = x.reshape(m, d)
    h_bf = h.astype(BF16)
    for l in range(depth):
        qkv = _matmul(h_bf, w_in, l, 0, 3 * bw, *MATMUL_TILE, BF16, "proj_qkv")
        rest = _matmul(h_bf, w_in, l, 3 * bw, 5 * bw, *MATMUL_TILE, F32, "proj_rest")
        o_a = _attention(qkv, batch, seq, heads, *ATTN_TILE)
        o_bc = _gmlp_conv(rest, gm_g3, gm_b3, w_spatial, b_sp_bcast, conv_w, l, seq, bw, TOKEN_TILE)
        merged = _merge(h_bf, o_a, o_bc, w_gate, b_gate4, w_branch, l, *MERGE_TILE)
        mix = _matmul(merged, w_out, l, 0, d, *MATMUL_TILE, F32, "out_proj")
        h, h_bf, h_pk = _residual_ln(h, mix, ln1_g3, ln1_b3, l, alpha, TOKEN_TILE)

        idx, rank, wts, counts = _router(h, w_router, b_router3, l, ROUTER_TILE)
        dest2d, tile_expert, n_used, next_used = _route_plan(counts, idx, rank, tile, n_tiles)
        dest = dest2d.reshape(-1)
        src = _inverse_map(dest, n_rows)
        y_sorted = _experts(tile_expert, n_used, src, next_used, h_pk, w_e_gate, w_e_up, w_e_down, l, n_tiles, tile)
        act_s = _swiglu(h_bf, w_s_gate, w_s_up, l, *SWIGLU_TILE)
        shared = _matmul(act_s, w_s_down, l, 0, d, *SHARED_DOWN_TILE, F32, "shared_down")
        h, h_bf = _combine(dest, y_sorted, h, shared, wts, ln2_g3, ln2_b3, l, alpha, COMBINE_TILE)
    return h.reshape(batch, seq, d)
```

```python
import functools

import jax
import jax.numpy as jnp
from jax import lax
from jax.experimental import pallas as pl
from jax.experimental.pallas import tpu as pltpu

F32 = jnp.float32
BF16 = jnp.bfloat16
I32 = jnp.int32

CHUNK = 64
HEAD_DIM = 128
GM_SPAN = 128
GM_GROUP_DIM = 128
IN_PARTS = 8
N_BRANCH = 3
TOP_K = 6
ROUTED_SCALE = 2.5
LN_EPS = 1e-5
LOG2_E = 1.4426950408889634

V7X_VMEM_LIMIT_BYTES = 56 * 1024 * 1024
LANES = 128
SUBLANES = 8
CAST_ROWS = 256
EXPERT_ROW_TILE = 256
GATHER_SLOTS = 3
IDX_LANES = 8
MATMUL_TILE = (1024, 512)
SHARED_DOWN_TILE = (2048, 1024)
SWIGLU_TILE = (1024, 256)
MERGE_TILE = (512, 256)
ATTN_TILE = (512, 512, 2)
TOKEN_TILE = 256
ROUTER_TILE = 512
COMBINE_TILE = 128


def _params(sem):
    return pltpu.CompilerParams(dimension_semantics=sem, vmem_limit_bytes=V7X_VMEM_LIMIT_BYTES)


def _cast_rows_to_bf16(src_ref, dst_ref):
    k = src_ref.shape[0]
    rows = min(CAST_ROWS, k)

    def body(i, c):
        r = pl.multiple_of(i * rows, rows)
        dst_ref[pl.ds(r, rows), :] = src_ref[pl.ds(r, rows), :].astype(BF16)
        return c

    lax.fori_loop(0, k // rows, body, 0)


def _dot(a, b):
    return jnp.dot(a, b, preferred_element_type=F32)


PACK_COLS = 2 * LANES
HIGH_HALF = -65536


def _pack_bf16_pairs(lo, hi):
    lo_bits = lax.bitcast_convert_type(lo.astype(BF16).astype(F32), I32)
    hi_bits = lax.bitcast_convert_type(hi.astype(BF16).astype(F32), I32)
    return hi_bits | lax.shift_right_logical(lo_bits, 16)


def _unpack_bf16_pairs(word):
    lo = lax.bitcast_convert_type(word << 16, F32)
    hi = lax.bitcast_convert_type(word & HIGH_HALF, F32)
    return lo, hi


def _gather_pitch(packs):
    pitch = -(-packs // SUBLANES) * SUBLANES
    return pitch + SUBLANES if pitch % (2 * SUBLANES) == 0 else pitch


def _store_packed_rows(values, pk_ref, before_chunk=None):
    rows, cols = values.shape
    packs = cols // PACK_COLS
    for j in range(packs):
        if before_chunk is not None:
            before_chunk(j)
        c0 = j * PACK_COLS
        word = _pack_bf16_pairs(values[:, c0:c0 + LANES], values[:, c0 + LANES:c0 + PACK_COLS])
        pk_ref[pl.ds(j, rows, stride=packs), :] = word


def _mm_kernel(a_ref, w_ref, o_ref, wbf_ref):
    @pl.when(pl.program_id(1) == 0)
    def _():
        _cast_rows_to_bf16(w_ref, wbf_ref)

    o_ref[...] = _dot(a_ref[...], wbf_ref[...]).astype(o_ref.dtype)


def _matmul(a, w, layer, col_off, n_cols, tm, tn, out_dtype, name):
    m, k = a.shape
    tm = min(tm, m)
    while n_cols % tn or col_off % tn:
        tn //= 2
    col_block_off = col_off // tn
    return pl.pallas_call(
        _mm_kernel,
        grid=(n_cols // tn, m // tm),
        in_specs=[
            pl.BlockSpec((tm, k), lambda n, i: (i, 0)),
            pl.BlockSpec((None, k, tn), lambda n, i: (layer, 0, n + col_block_off)),
        ],
        out_specs=pl.BlockSpec((tm, tn), lambda n, i: (i, n)),
        out_shape=jax.ShapeDtypeStruct((m, n_cols), out_dtype),
        scratch_shapes=[pltpu.VMEM((k, tn), BF16)],
        compiler_params=_params(("arbitrary", "arbitrary")),
        name=name,
    )(a, w)


def _attn_kernel(q_ref, k_ref, v_ref, o_ref, *, tq, kb, ks, hp, scale):
    qi = pl.program_id(2)
    n_sub = ks // kb
    n_steps = ((qi + 1) * tq + ks - 1) // ks
    t_pos = lax.broadcasted_iota(I32, (tq, kb), 0) + qi * tq
    s_loc = lax.broadcasted_iota(I32, (tq, kb), 1)
    r = lax.broadcasted_iota(I32, (2 * kb, 2 * kb), 0)
    c = lax.broadcasted_iota(I32, (2 * kb, 2 * kb), 1)
    r = jnp.where(r >= kb, r - kb, r)
    suffix = jnp.where((c >= kb) | (r > c), 1.0, 0.0).astype(BF16)

    def step(j, carry, masked):
        k0 = pl.multiple_of(j * ks, ks)
        return tuple(head_step(h, k0, carry[h], masked) for h in range(hp))

    def head_step(h, k0, carry, masked):
        acc, later_sum = carry
        cols = slice(h * HEAD_DIM, (h + 1) * HEAD_DIM)
        q = q_ref[:, cols]
        kj = k_ref[pl.ds(k0, ks), cols]
        vj = v_ref[pl.ds(k0, ks), cols]
        z = lax.dot_general(q, kj, (((1,), (1,)), ((), ())), preferred_element_type=F32) * (scale * LOG2_E)
        weights = [None] * n_sub
        for s in reversed(range(n_sub)):
            zs = z[:, s * kb:(s + 1) * kb]
            softplus = jnp.log(1.0 + jnp.exp2(jnp.minimum(zs, -zs))) * LOG2_E
            log_beta = jnp.minimum(zs, 0.0) - softplus
            log_one_minus = log_beta - zs
            if masked:
                strict = (s_loc + (k0 + s * kb)) < t_pos
                log_one_minus = jnp.where(strict, log_one_minus, 0.0)
            hi = log_one_minus.astype(BF16)
            lo = (log_one_minus - hi.astype(F32)).astype(BF16)
            sums = _dot(jnp.concatenate([hi, lo], axis=1), suffix)
            a = jnp.exp2(log_beta + (sums[:, :kb] + later_sum))
            if masked:
                a = jnp.where(strict, a, 0.0)
            weights[s] = a.astype(BF16)
            later_sum = later_sum + sums[:, kb:]
        acc = acc + _dot(jnp.concatenate(weights, axis=1), vj)
        return acc, later_sum

    carry = tuple((jnp.zeros((tq, HEAD_DIM), F32), jnp.zeros((tq, kb), F32)) for _ in range(hp))
    carry = step(n_steps - 1, carry, True)
    carry = lax.fori_loop(0, n_steps - 1, lambda t, cr: step(n_steps - 2 - t, cr, False), carry)
    for h in range(hp):
        o_ref[:, h * HEAD_DIM:(h + 1) * HEAD_DIM] = carry[h][0].astype(o_ref.dtype)


def _attention(qkv, batch, seq, heads, tq, ks, hp):
    m = qkv.shape[0]
    tq = min(tq, seq)
    ks = min(ks, seq)
    hp = min(hp, heads)
    assert ks % tq == 0 and heads % hp == 0
    kb = HEAD_DIM
    nq = seq // tq
    groups = heads // hp
    width = hp * HEAD_DIM
    kern = functools.partial(_attn_kernel, tq=tq, kb=kb, ks=ks, hp=hp, scale=HEAD_DIM ** -0.5)
    return pl.pallas_call(
        kern,
        grid=(batch, groups, nq),
        in_specs=[
            pl.BlockSpec((tq, width), lambda b, g, i: (b * nq + i, g)),
            pl.BlockSpec((seq, width), lambda b, g, i: (b, groups + g)),
            pl.BlockSpec((seq, width), lambda b, g, i: (b, 2 * groups + g)),
        ],
        out_specs=pl.BlockSpec((tq, width), lambda b, g, i: (b * nq + i, g)),
        out_shape=jax.ShapeDtypeStruct((m, heads * HEAD_DIM), BF16),
        compiler_params=_params(("arbitrary", "arbitrary", "arbitrary")),
        name="sb_attention",
    )(qkv, qkv, qkv)


def _gelu_tanh(x):
    return 0.5 * x * (1.0 + jnp.tanh(0.7978845608028654 * (x + 0.044715 * (x * x * x))))


def _gmlp_conv_kernel(gu_ref, gv_ref, cb_ref, cc_ref, cx_ref, ccp_ref, cxp_ref, lng_ref, lnb_ref,
                      wsp_ref, bsp_ref, cw_ref, o_ref, *, tt, seq, groups):
    i = pl.program_id(0)
    z = cc_ref[...] * cx_ref[...]
    first = (i * tt) % seq == 0
    zp = jnp.where(first, 0.0, ccp_ref[...] * cxp_ref[...])
    row = lax.broadcasted_iota(I32, z.shape, 0)
    z1 = jnp.where(row == 0, zp[SUBLANES - 1:SUBLANES, :], pltpu.roll(z, 1, 0))
    z2 = pltpu.roll(z, 2, 0)
    z2 = jnp.where(row == 0, zp[SUBLANES - 2:SUBLANES - 1, :], z2)
    z2 = jnp.where(row == 1, zp[SUBLANES - 1:SUBLANES, :], z2)
    y = cw_ref[0:1, :] * z2 + cw_ref[1:2, :] * z1 + cw_ref[2:3, :] * z
    o_ref[1] = (cb_ref[...] * y).astype(o_ref.dtype)

    v = _gelu_tanh(gv_ref[...])
    mu = jnp.mean(v, axis=-1, keepdims=True)
    var = jnp.mean(jnp.square(v - mu), axis=-1, keepdims=True)
    vn = ((v - mu) * lax.rsqrt(var + LN_EPS) * lng_ref[...] + lnb_ref[...]).astype(BF16)
    u = _gelu_tanh(gu_ref[...])
    tr = lax.broadcasted_iota(I32, (GM_SPAN, GM_SPAN), 0) // CHUNK
    sc = lax.broadcasted_iota(I32, (GM_SPAN, GM_SPAN), 1) // CHUNK
    causal = tr >= sc
    for g in range(groups):
        wm = jnp.where(causal, wsp_ref[g], 0.0).astype(BF16)
        cols = slice(g * GM_GROUP_DIM, (g + 1) * GM_GROUP_DIM)
        for s in range(tt // GM_SPAN):
            rows = slice(s * GM_SPAN, (s + 1) * GM_SPAN)
            f = _dot(wm, vn[rows, cols]) + bsp_ref[g]
            o_ref[0, rows, cols] = (u[rows, cols] * f).astype(o_ref.dtype)


def _gmlp_conv(rest, gm_ln_g, gm_ln_b, w_spatial, b_spatial_bcast, conv_w, layer, seq, bw, tt):
    m = rest.shape[0]
    tt = min(tt, seq)
    groups = bw // GM_GROUP_DIM
    halo_blocks = tt // SUBLANES
    kern = functools.partial(_gmlp_conv_kernel, tt=tt, seq=seq, groups=groups)

    def part(p):
        return pl.BlockSpec((tt, bw), lambda i: (i, p))

    def halo(p):
        return pl.BlockSpec((SUBLANES, bw), lambda i: (jnp.maximum(i * halo_blocks - 1, 0), p))

    return pl.pallas_call(
        kern,
        grid=(m // tt,),
        in_specs=[
            part(0), part(1), part(2), part(3), part(4), halo(3), halo(4),
            pl.BlockSpec((None, 1, bw), lambda i: (layer, 0, 0)),
            pl.BlockSpec((None, 1, bw), lambda i: (layer, 0, 0)),
            pl.BlockSpec((None, groups, GM_SPAN, GM_SPAN), lambda i: (layer, 0, 0, 0)),
            pl.BlockSpec((None, groups, GM_SPAN, GM_SPAN), lambda i: (layer, 0, 0, 0)),
            pl.BlockSpec((None, 3, bw), lambda i: (layer, 0, 0)),
        ],
        out_specs=pl.BlockSpec((2, tt, bw), lambda i: (0, i, 0)),
        out_shape=jax.ShapeDtypeStruct((2, m, bw), BF16),
        compiler_params=_params(("arbitrary",)),
        name="gmlp_conv",
    )(rest, rest, rest, rest, rest, rest, rest, gm_ln_g, gm_ln_b, w_spatial, b_spatial_bcast, conv_w)


def _merge_kernel(h_ref, oa_ref, obc_ref, wg_ref, bg_ref, wb_ref, o_ref, wg_bf, wb_bf):
    @pl.when(pl.program_id(1) == 0)
    def _():
        for i in range(N_BRANCH):
            _cast_rows_to_bf16(wg_ref.at[i], wg_bf.at[i])
            _cast_rows_to_bf16(wb_ref.at[i], wb_bf.at[i])

    h = h_ref[...]
    merged = None
    for i in range(N_BRANCH):
        gate = jax.nn.sigmoid(_dot(h, wg_bf[i]) + bg_ref[i])
        branch = oa_ref[...] if i == 0 else obc_ref[i - 1]
        term = gate * _dot(branch, wb_bf[i])
        merged = term if merged is None else merged + term
    o_ref[...] = merged.astype(o_ref.dtype)


def _merge(h_bf, o_a, o_bc, w_gate, b_gate4, w_branch, layer, tm, tn):
    m, d = h_bf.shape
    bw = o_a.shape[1]
    tm = min(tm, m)
    tn = min(tn, d)
    return pl.pallas_call(
        _merge_kernel,
        grid=(d // tn, m // tm),
        in_specs=[
            pl.BlockSpec((tm, d), lambda n, i: (i, 0)),
            pl.BlockSpec((tm, bw), lambda n, i: (i, 0)),
            pl.BlockSpec((2, tm, bw), lambda n, i: (0, i, 0)),
            pl.BlockSpec((None, N_BRANCH, d, tn), lambda n, i: (layer, 0, 0, n)),
            pl.BlockSpec((None, N_BRANCH, 1, tn), lambda n, i: (layer, 0, 0, n)),
            pl.BlockSpec((None, N_BRANCH, bw, tn), lambda n, i: (layer, 0, 0, n),
                         pipeline_mode=pl.Buffered(1)),
        ],
        out_specs=pl.BlockSpec((tm, tn), lambda n, i: (i, n)),
        out_shape=jax.ShapeDtypeStruct((m, d), BF16),
        scratch_shapes=[pltpu.VMEM((N_BRANCH, d, tn), BF16), pltpu.VMEM((N_BRANCH, bw, tn), BF16)],
        compiler_params=_params(("arbitrary", "arbitrary")),
        name="branch_merge",
    )(h_bf, o_a, o_bc, w_gate, b_gate4, w_branch)


def _layer_norm_rows(y, g, b):
    mu = jnp.mean(y, axis=-1, keepdims=True)
    var = jnp.mean(jnp.square(y - mu), axis=-1, keepdims=True)
    return (y - mu) * lax.rsqrt(var + LN_EPS) * g + b


def _residual_ln_kernel(h_ref, f_ref, g_ref, b_ref, o_ref, obf_ref, opk_ref, *, alpha):
    out = _layer_norm_rows(alpha * h_ref[...] + f_ref[...], g_ref[...], b_ref[...])
    o_ref[...] = out
    obf_ref[...] = out.astype(BF16)
    _store_packed_rows(out, opk_ref)


def _residual_ln(h, f, g, b, layer, alpha, tm):
    m, d = h.shape
    tm = min(tm, m)
    row = pl.BlockSpec((tm, d), lambda i: (i, 0))
    vec = pl.BlockSpec((None, 1, d), lambda i: (layer, 0, 0))
    packs = d // PACK_COLS
    packed = pl.BlockSpec((tm * packs, LANES), lambda i: (i, 0))
    return pl.pallas_call(
        functools.partial(_residual_ln_kernel, alpha=alpha),
        grid=(m // tm,),
        in_specs=[row, row, vec, vec],
        out_specs=[row, row, packed],
        out_shape=[jax.ShapeDtypeStruct((m, d), F32), jax.ShapeDtypeStruct((m, d), BF16),
                   jax.ShapeDtypeStruct((m * packs, LANES), I32)],
        compiler_params=_params(("arbitrary",)),
        name="residual_ln",
    )(h, f, g, b)


def _router_kernel(h_ref, w_ref, b_ref, idx_ref, rank_ref, wts_ref, cnt_ref, run_ref, *, tr, n_exp):
    @pl.when(pl.program_id(0) == 0)
    def _():
        run_ref[...] = jnp.zeros_like(run_ref)

    h = h_ref[...]
    w = w_ref[...]
    h_hi = h.astype(BF16)
    h_lo = (h - h_hi.astype(F32)).astype(BF16)
    w_hi = w.astype(BF16)
    w_lo = (w - w_hi.astype(F32)).astype(BF16)
    logits = _dot(h_hi, w_hi) + (_dot(h_hi, w_lo) + _dot(h_lo, w_hi))
    scores = jax.nn.sigmoid(logits)
    sel = scores + b_ref[...]
    lane = lax.broadcasted_iota(I32, (tr, n_exp), 1).astype(F32)
    out_lane = lax.broadcasted_iota(I32, (tr, IDX_LANES), 1)
    mask = jnp.zeros((tr, n_exp), F32)
    idx_out = jnp.zeros((tr, IDX_LANES), F32)
    picks = []
    for k in range(TOP_K):
        best = jnp.max(sel, axis=-1, keepdims=True)
        pick = jnp.min(jnp.where(sel == best, lane, float(n_exp)), axis=-1, keepdims=True)
        hit = lane == pick
        mask = jnp.where(hit, 1.0, mask)
        sel = jnp.where(hit, -jnp.inf, sel)
        idx_out = jnp.where(out_lane == k, pick, idx_out)
        picks.append(hit)
    chosen_scores = mask * scores
    gates = chosen_scores / jnp.sum(chosen_scores, axis=-1, keepdims=True) * ROUTED_SCALE

    rr = lax.broadcasted_iota(I32, (tr, tr), 0)
    cc = lax.broadcasted_iota(I32, (tr, tr), 1)
    lower = jnp.where(rr > cc, 1.0, 0.0).astype(BF16)
    rank = _dot(lower, mask.astype(BF16)) + run_ref[...]
    run_ref[...] = run_ref[...] + jnp.sum(mask, axis=0, keepdims=True)
    cnt_ref[...] = run_ref[...].astype(I32)

    rank_out = jnp.zeros((tr, IDX_LANES), F32)
    wts_out = jnp.zeros((tr, IDX_LANES), F32)
    for k in range(TOP_K):
        rank_out = jnp.where(out_lane == k, jnp.sum(jnp.where(picks[k], rank, 0.0), axis=-1, keepdims=True), rank_out)
        wts_out = jnp.where(out_lane == k, jnp.sum(jnp.where(picks[k], gates, 0.0), axis=-1, keepdims=True), wts_out)
    idx_ref[...] = idx_out.astype(I32)
    rank_ref[...] = rank_out.astype(I32)
    wts_ref[...] = wts_out


def _router(h, w_router, b_router3, layer, tr):
    m, d = h.shape
    n_exp = w_router.shape[-1]
    tr = min(tr, m)
    lists = pl.BlockSpec((tr, IDX_LANES), lambda i: (i, 0))
    return pl.pallas_call(
        functools.partial(_router_kernel, tr=tr, n_exp=n_exp),
        grid=(m // tr,),
        in_specs=[
            pl.BlockSpec((tr, d), lambda i: (i, 0)),
            pl.BlockSpec((None, d, n_exp), lambda i: (layer, 0, 0)),
            pl.BlockSpec((None, 1, n_exp), lambda i: (layer, 0, 0)),
        ],
        out_specs=[lists, lists, lists, pl.BlockSpec((1, n_exp), lambda i: (0, 0))],
        out_shape=[
            jax.ShapeDtypeStruct((m, IDX_LANES), I32),
            jax.ShapeDtypeStruct((m, IDX_LANES), I32),
            jax.ShapeDtypeStruct((m, IDX_LANES), F32),
            jax.ShapeDtypeStruct((1, n_exp), I32),
        ],
        scratch_shapes=[pltpu.VMEM((1, n_exp), F32)],
        compiler_params=_params(("arbitrary",)),
        name="router_topk",
    )(h, w_router, b_router3)


def _inverse_kernel(dest_ref, zeros_hbm, src_ref, sem, *, n_pairs):
    clear = pltpu.make_async_copy(zeros_hbm, src_ref, sem)
    clear.start()
    clear.wait()

    def place(t, c):
        for k in range(TOP_K):
            src_ref[dest_ref[t * TOP_K + k]] = t
        return c

    lax.fori_loop(0, n_pairs // TOP_K, place, 0, unroll=2)


def _inverse_map(dest_flat, n_rows):
    n_pairs = dest_flat.shape[0]
    smem = pl.BlockSpec(memory_space=pltpu.SMEM)
    return pl.pallas_call(
        functools.partial(_inverse_kernel, n_pairs=n_pairs),
        in_specs=[smem, pl.BlockSpec(memory_space=pl.ANY)],
        out_specs=smem,
        out_shape=jax.ShapeDtypeStruct((n_rows,), I32),
        scratch_shapes=[pltpu.SemaphoreType.DMA(())],
        name="inverse_map",
    )(dest_flat, jnp.zeros((n_rows,), I32))


def _route_plan_kernel(cnt_ref, idx_ref, rank_ref, dest_ref, te_ref, nu_ref, nxt_ref, off_ref, *,
                       n_exp, tile, n_tiles):
    shift = tile.bit_length() - 1

    def link(t, following):
        e = n_exp - 1 - t
        nxt_ref[e] = following
        return jnp.where(cnt_ref[e] > 0, e, following)

    lax.fori_loop(0, n_exp, link, jnp.int32(-1))

    def per_expert(e, carry):
        tiles_before, last_owner = carry
        nt = lax.shift_right_logical(cnt_ref[e] + (tile - 1), shift)
        off_ref[e] = tiles_before * tile

        def fill(q, c):
            te_ref[tiles_before + q] = e
            return c

        lax.fori_loop(0, nt, fill, 0)
        return tiles_before + nt, jnp.where(nt > 0, e, last_owner)

    used, last_owner = lax.fori_loop(0, n_exp, per_expert, (jnp.int32(0), jnp.int32(0)))
    nu_ref[0] = used

    def fill_tail(q, c):
        te_ref[q] = last_owner
        return c

    lax.fori_loop(used, n_tiles, fill_tail, 0)

    idx = idx_ref[...]
    offset = lax.fori_loop(0, n_exp, lambda e, acc: jnp.where(idx == e, off_ref[e], acc),
                           jnp.zeros_like(idx))
    dest_ref[...] = offset + rank_ref[...]


def _route_plan(counts, idx, rank, tile, n_tiles):
    n_exp = counts.shape[-1]
    pairs = idx[:, :TOP_K].reshape(-1, LANES)
    ranks = rank[:, :TOP_K].reshape(-1, LANES)
    smem = pl.BlockSpec(memory_space=pltpu.SMEM)
    vmem = pl.BlockSpec(memory_space=pltpu.VMEM)
    assert tile & (tile - 1) == 0
    return pl.pallas_call(
        functools.partial(_route_plan_kernel, n_exp=n_exp, tile=tile, n_tiles=n_tiles),
        in_specs=[smem, vmem, vmem],
        out_specs=[vmem, smem, smem, smem],
        out_shape=[jax.ShapeDtypeStruct(pairs.shape, I32), jax.ShapeDtypeStruct((n_tiles,), I32),
                   jax.ShapeDtypeStruct((1,), I32), jax.ShapeDtypeStruct((n_exp,), I32)],
        scratch_shapes=[pltpu.SMEM((n_exp,), I32)],
        name="route_plan",
    )(counts.reshape(n_exp), pairs, ranks)


def _expert_kernel(te_ref, nu_ref, src_ref, nxt_ref, h_hbm, wg_hbm, wu_hbm, wd_hbm, o_ref,
                   xbuf, x_bf, wg_f32, wu_f32, wd_f32, wg_bf, wu_bf, wd_bf, wslot, sem, wsem,
                   *, tile, layer):
    i = pl.program_id(0)
    n_used = nu_ref[0]
    slot = lax.rem(i, GATHER_SLOTS)

    packs = x_bf.shape[1] // PACK_COLS
    pitch = xbuf.shape[0] // (GATHER_SLOTS * tile)

    def weight_copies(expert, s):
        return [pltpu.make_async_copy(hbm.at[layer, expert], buf.at[s], wsem.at[s, n])
                for n, (hbm, buf) in enumerate(((wg_hbm, wg_f32), (wu_hbm, wu_f32), (wd_hbm, wd_f32)))]

    def start_row(tile_idx, s, r, priority):
        src_row = pl.multiple_of(src_ref[tile_idx * tile + r] * packs, packs)
        dst_row = pl.multiple_of((s * tile + r) * pitch, SUBLANES)
        pltpu.make_async_copy(h_hbm.at[pl.ds(src_row, packs)], xbuf.at[pl.ds(dst_row, packs)],
                              sem.at[s]).start(priority=priority)

    def wait_gather(s):
        dst_row = pl.multiple_of(s * (tile * pitch), SUBLANES)
        pltpu.make_async_copy(h_hbm.at[pl.ds(0, tile * packs)], xbuf.at[pl.ds(dst_row, tile * packs)],
                              sem.at[s]).wait()

    @pl.when(i == 0)
    def _():
        wslot[0] = 0
        for cp in weight_copies(te_ref[0], 0):
            cp.start(priority=1)

        for first in range(GATHER_SLOTS - 1):
            @pl.when(first < n_used)
            def _():
                def body(r2, c):
                    for p in range(2):
                        start_row(first, first, 2 * r2 + p, p)
                    return c
                lax.fori_loop(0, tile // 2, body, 0, unroll=4)

    def compute(prefetch):
        expert = te_ref[i]
        changed = jnp.logical_or(i == 0, expert != te_ref[jnp.maximum(i - 1, 0)])

        @pl.when(changed)
        def _():
            s = wslot[0]
            for cp in weight_copies(expert, s):
                cp.wait()
            following = nxt_ref[expert]

            @pl.when(following >= 0)
            def _():
                for cp in weight_copies(following, 1 - s):
                    cp.start(priority=1)

            _cast_rows_to_bf16(wg_f32.at[s], wg_bf)
            _cast_rows_to_bf16(wu_f32.at[s], wu_bf)
            _cast_rows_to_bf16(wd_f32.at[s], wd_bf)
            wslot[0] = 1 - s

        wait_gather(slot)
        rows_per_chunk = tile // (2 * packs)
        ahead = GATHER_SLOTS - 1
        ahead_slot = lax.rem(i + ahead, GATHER_SLOTS)

        def request_rows(chunk):
            if prefetch:
                for r in range(chunk * rows_per_chunk, (chunk + 1) * rows_per_chunk):
                    start_row(i + ahead, ahead_slot, r, r % 2)

        for j in range(packs):
            request_rows(j)
            lo, hi = _unpack_bf16_pairs(xbuf[pl.ds(slot * (tile * pitch) + j, tile, stride=pitch), :])
            c0 = j * PACK_COLS
            x_bf[:, c0:c0 + LANES] = lo.astype(BF16)
            x_bf[:, c0 + LANES:c0 + PACK_COLS] = hi.astype(BF16)
        x = x_bf[...]
        act = jax.nn.silu(_dot(x, wg_bf[...])) * _dot(x, wu_bf[...])
        _store_packed_rows(_dot(act.astype(BF16), wd_bf[...]), o_ref,
                           before_chunk=lambda j: request_rows(packs + j))

    @pl.when(i + (GATHER_SLOTS - 1) < n_used)
    def _():
        compute(True)

    @pl.when(jnp.logical_and(i < n_used, i + (GATHER_SLOTS - 1) >= n_used))
    def _():
        compute(False)

    @pl.when(i >= n_used)
    def _():
        o_ref[...] = jnp.zeros_like(o_ref)


def _experts(tile_expert, n_used, src, next_used, h_pk, w_e_gate, w_e_up, w_e_down, layer, n_tiles, tile):
    d, ff = w_e_gate.shape[-2:]
    packs = d // PACK_COLS
    hbm = pl.BlockSpec(memory_space=pl.ANY)

    grid_spec = pltpu.PrefetchScalarGridSpec(
        num_scalar_prefetch=4,
        grid=(n_tiles,),
        in_specs=[hbm, hbm, hbm, hbm],
        out_specs=pl.BlockSpec((tile * packs, LANES), lambda i, te, nu, s, nx: (i, 0)),
        scratch_shapes=[
            pltpu.VMEM((GATHER_SLOTS * tile * _gather_pitch(packs), LANES), I32),
            pltpu.VMEM((tile, d), BF16),
            pltpu.VMEM((2, d, ff), F32),
            pltpu.VMEM((2, d, ff), F32),
            pltpu.VMEM((2, ff, d), F32),
            pltpu.VMEM((d, ff), BF16),
            pltpu.VMEM((d, ff), BF16),
            pltpu.VMEM((ff, d), BF16),
            pltpu.SMEM((1,), I32),
            pltpu.SemaphoreType.DMA((GATHER_SLOTS,)),
            pltpu.SemaphoreType.DMA((2, 3)),
        ],
    )
    return pl.pallas_call(
        functools.partial(_expert_kernel, tile=tile, layer=layer),
        grid_spec=grid_spec,
        out_shape=jax.ShapeDtypeStruct((n_tiles * tile * packs, LANES), I32),
        compiler_params=_params(("arbitrary",)),
        name="expert_mlp",
    )(tile_expert, n_used, src, next_used, h_pk, w_e_gate, w_e_up, w_e_down)


def _swiglu_kernel(a_ref, wg_ref, wu_ref, o_ref, wg_bf, wu_bf):
    @pl.when(pl.program_id(1) == 0)
    def _():
        _cast_rows_to_bf16(wg_ref, wg_bf)
        _cast_rows_to_bf16(wu_ref, wu_bf)

    a = a_ref[...]
    o_ref[...] = (jax.nn.silu(_dot(a, wg_bf[...])) * _dot(a, wu_bf[...])).astype(o_ref.dtype)


def _swiglu(a, w_gate, w_up, layer, tm, tn):
    m, k = a.shape
    n = w_gate.shape[-1]
    tm = min(tm, m)
    tn = min(tn, n)
    wspec = pl.BlockSpec((None, k, tn), lambda j, i: (layer, 0, j))
    return pl.pallas_call(
        _swiglu_kernel,
        grid=(n // tn, m // tm),
        in_specs=[pl.BlockSpec((tm, k), lambda j, i: (i, 0)), wspec, wspec],
        out_specs=pl.BlockSpec((tm, tn), lambda j, i: (i, j)),
        out_shape=jax.ShapeDtypeStruct((m, n), BF16),
        scratch_shapes=[pltpu.VMEM((k, tn), BF16), pltpu.VMEM((k, tn), BF16)],
        compiler_params=_params(("arbitrary", "arbitrary")),
        name="shared_swiglu",
    )(a, w_gate, w_up)


def _combine_kernel(dest_ref, y_hbm, h_ref, sh_ref, wts_ref, g_ref, b_ref, o_ref, obf_ref,
                    ybuf, routed, sem, *, tc, alpha):
    i = pl.program_id(0)
    n_steps = pl.num_programs(0)
    slot = i % 2

    packs = routed.shape[1] // PACK_COLS
    pitch = ybuf.shape[0] // (2 * TOP_K * tc)

    def start_row(step, s, t, k):
        src_row = pl.multiple_of(dest_ref[(step * tc + t) * TOP_K + k] * packs, packs)
        dst_row = pl.multiple_of(((s * TOP_K + k) * tc + t) * pitch, SUBLANES)
        pltpu.make_async_copy(y_hbm.at[pl.ds(src_row, packs)], ybuf.at[pl.ds(dst_row, packs)],
                              sem.at[s]).start(priority=k % 2)

    def wait_gather(s):
        dst_row = pl.multiple_of(s * (TOP_K * tc * pitch), SUBLANES)
        pltpu.make_async_copy(y_hbm.at[pl.ds(0, TOP_K * tc * packs)],
                              ybuf.at[pl.ds(dst_row, TOP_K * tc * packs)], sem.at[s]).wait()

    @pl.when(i == 0)
    def _():
        def body(t, c):
            for k in range(TOP_K):
                start_row(0, 0, t, k)
            return c
        lax.fori_loop(0, tc, body, 0, unroll=2)

    def compute(prefetch_next):
        wait_gather(slot)
        wts = wts_ref[...]
        gate = [jnp.broadcast_to(wts[:, k:k + 1], (tc, LANES)) for k in range(TOP_K)]
        tokens_per_chunk = tc // packs
        for j in range(packs):
            if prefetch_next:
                for t in range(j * tokens_per_chunk, (j + 1) * tokens_per_chunk):
                    for k in range(TOP_K):
                        start_row(i + 1, 1 - slot, t, k)
            acc_lo = acc_hi = None
            for k in range(TOP_K):
                base = (slot * TOP_K + k) * (tc * pitch) + j
                lo, hi = _unpack_bf16_pairs(ybuf[pl.ds(base, tc, stride=pitch), :])
                acc_lo = gate[k] * lo if acc_lo is None else acc_lo + gate[k] * lo
                acc_hi = gate[k] * hi if acc_hi is None else acc_hi + gate[k] * hi
            c0 = j * PACK_COLS
            routed[:, c0:c0 + LANES] = acc_lo
            routed[:, c0 + LANES:c0 + PACK_COLS] = acc_hi
        y = alpha * h_ref[...] + (routed[...] + sh_ref[...])
        out = _layer_norm_rows(y, g_ref[...], b_ref[...])
        o_ref[...] = out
        obf_ref[...] = out.astype(BF16)

    @pl.when(i + 1 < n_steps)
    def _():
        compute(True)

    @pl.when(i + 1 == n_steps)
    def _():
        compute(False)


def _combine(dest_flat, y_sorted, h, shared, wts, g, b, layer, alpha, tc):
    m, d = h.shape
    tc = min(tc, m)
    row = pl.BlockSpec((tc, d), lambda i, dest: (i, 0))
    vec = pl.BlockSpec((None, 1, d), lambda i, dest: (layer, 0, 0))
    grid_spec = pltpu.PrefetchScalarGridSpec(
        num_scalar_prefetch=1,
        grid=(m // tc,),
        in_specs=[
            pl.BlockSpec(memory_space=pl.ANY),
            row, row,
            pl.BlockSpec((tc, IDX_LANES), lambda i, dest: (i, 0)),
            vec, vec,
        ],
        out_specs=[row, row],
        scratch_shapes=[pltpu.VMEM((2 * TOP_K * tc * _gather_pitch(d // PACK_COLS), LANES), I32),
                        pltpu.VMEM((tc, d), F32), pltpu.SemaphoreType.DMA((2,))],
    )
    return pl.pallas_call(
        functools.partial(_combine_kernel, tc=tc, alpha=alpha),
        grid_spec=grid_spec,
        out_shape=[jax.ShapeDtypeStruct((m, d), F32), jax.ShapeDtypeStruct((m, d), BF16)],
        compiler_params=_params(("arbitrary",)),
        name="moe_combine_ln",
    )(dest_flat, y_sorted, h, shared, wts, g, b)


def kernel(x, w_in, w_gate, b_gate, w_branch, w_out, gm_ln_g, gm_ln_b, w_spatial, b_spatial, conv_w,
           ln1_g, ln1_b, w_router, b_router, w_e_gate, w_e_up, w_e_down, w_s_gate, w_s_up, w_s_down,
           ln2_g, ln2_b):
    batch, seq, d = x.shape
    depth = w_in.shape[0]
    m = batch * seq
    bw = d // 4
    heads = bw // HEAD_DIM
    n_exp = w_router.shape[-1]
    alpha = float((2 * depth) ** 0.25)
    tile = EXPERT_ROW_TILE
    n_tiles = (m * TOP_K + n_exp * (tile - 1)) // tile + 1
    n_rows = n_tiles * tile

    b_gate4 = b_gate.reshape(depth, N_BRANCH, 1, d)
    gm_g3 = gm_ln_g.reshape(depth, 1, bw)
    gm_b3 = gm_ln_b.reshape(depth, 1, bw)
    b_sp_bcast = jnp.broadcast_to(b_spatial[..., None], b_spatial.shape + (GM_SPAN,))
    ln1_g3, ln1_b3 = ln1_g.reshape(depth, 1, d), ln1_b.reshape(depth, 1, d)
    ln2_g3, ln2_b3 = ln2_g.reshape(depth, 1, d), ln2_b.reshape(depth, 1, d)
    b_router3 = b_router.reshape(depth, 1, n_exp)

    h = x.reshape(m, d)
    h_bf = h.astype(BF16)
    for l in range(depth):
        qkv = _matmul(h_bf, w_in, l, 0, 3 * bw, *MATMUL_TILE, BF16, "proj_qkv")
        rest = _matmul(h_bf, w_in, l, 3 * bw, 5 * bw, *MATMUL_TILE, F32, "proj_rest")
        o_a = _attention(qkv, batch, seq, heads, *ATTN_TILE)
        o_bc = _gmlp_conv(rest, gm_g3, gm_b3, w_spatial, b_sp_bcast, conv_w, l, seq, bw, TOKEN_TILE)
        merged = _merge(h_bf, o_a, o_bc, w_gate, b_gate4, w_branch, l, *MERGE_TILE)
        mix = _matmul(merged, w_out, l, 0, d, *MATMUL_TILE, F32, "out_proj")
        h, h_bf, h_pk = _residual_ln(h, mix, ln1_g3, ln1_b3, l, alpha, TOKEN_TILE)

        idx, rank, wts, counts = _router(h, w_router, b_router3, l, ROUTER_TILE)
        dest2d, tile_expert, n_used, next_used = _route_plan(counts, idx, rank, tile, n_tiles)
        dest = dest2d.reshape(-1)
        src = _inverse_map(dest, n_rows)
        y_sorted = _experts(tile_expert, n_used, src, next_used, h_pk, w_e_gate, w_e_up, w_e_down, l, n_tiles, tile)
        act_s = _swiglu(h_bf, w_s_gate, w_s_up, l, *SWIGLU_TILE)
        shared = _matmul(act_s, w_s_down, l, 0, d, *SHARED_DOWN_TILE, F32, "shared_down")
        h, h_bf = _combine(dest, y_sorted, h, shared, wts, ln2_g3, ln2_b3, l, alpha, COMBINE_TILE)
    return h.reshape(batch, seq, d)
```

```python
import functools

import jax
import jax.numpy as jnp
from jax import lax
from jax.experimental import pallas as pl
from jax.experimental.pallas import tpu as pltpu

F32 = jnp.float32
BF16 = jnp.bfloat16
I32 = jnp.int32

CHUNK = 64
HEAD_DIM = 128
GM_SPAN = 128
GM_GROUP_DIM = 128
IN_PARTS = 8
N_BRANCH = 3
TOP_K = 6
ROUTED_SCALE = 2.5
LN_EPS = 1e-5
LOG2_E = 1.4426950408889634

V7X_VMEM_LIMIT_BYTES = 56 * 1024 * 1024
LANES = 128
SUBLANES = 8
CAST_ROWS = 256
EXPERT_ROW_TILE = 256
GATHER_SLOTS = 3
IDX_LANES = 8
MATMUL_TILE = (1024, 512)
SHARED_DOWN_TILE = (2048, 1024)
SWIGLU_TILE = (1024, 256)
MERGE_TILE = (512, 256)
ATTN_TILE = (512, 512, 4)
TOKEN_TILE = 256
ROUTER_TILE = 512
COMBINE_TILE = 128


def _params(sem):
    return pltpu.CompilerParams(dimension_semantics=sem, vmem_limit_bytes=V7X_VMEM_LIMIT_BYTES)


def _cast_rows_to_bf16(src_ref, dst_ref):
    k = src_ref.shape[0]
    rows = min(CAST_ROWS, k)

    def body(i, c):
        r = pl.multiple_of(i * rows, rows)
        dst_ref[pl.ds(r, rows), :] = src_ref[pl.ds(r, rows), :].astype(BF16)
        return c

    lax.fori_loop(0, k // rows, body, 0)


def _dot(a, b):
    return jnp.dot(a, b, preferred_element_type=F32)


PACK_COLS = 2 * LANES
HIGH_HALF = -65536


def _pack_bf16_pairs(lo, hi):
    lo_bits = lax.bitcast_convert_type(lo.astype(BF16).astype(F32), I32)
    hi_bits = lax.bitcast_convert_type(hi.astype(BF16).astype(F32), I32)
    return hi_bits | lax.shift_right_logical(lo_bits, 16)


def _unpack_bf16_pairs(word):
    lo = lax.bitcast_convert_type(word << 16, F32)
    hi = lax.bitcast_convert_type(word & HIGH_HALF, F32)
    return lo, hi


def _gather_pitch(packs):
    pitch = -(-packs // SUBLANES) * SUBLANES
    return pitch + SUBLANES if pitch % (2 * SUBLANES) == 0 else pitch


def _store_packed_rows(values, pk_ref):
    rows, cols = values.shape
    packs = cols // PACK_COLS
    for j in range(packs):
        c0 = j * PACK_COLS
        word = _pack_bf16_pairs(values[:, c0:c0 + LANES], values[:, c0 + LANES:c0 + PACK_COLS])
        pk_ref[pl.ds(j, rows, stride=packs), :] = word


def _mm_kernel(a_ref, w_ref, o_ref, wbf_ref):
    @pl.when(pl.program_id(1) == 0)
    def _():
        _cast_rows_to_bf16(w_ref, wbf_ref)

    o_ref[...] = _dot(a_ref[...], wbf_ref[...]).astype(o_ref.dtype)


def _mm_residual_kernel(a_ref, w_ref, r_ref, o_ref, wbf_ref, *, alpha):
    @pl.when(pl.program_id(1) == 0)
    def _():
        _cast_rows_to_bf16(w_ref, wbf_ref)

    o_ref[...] = alpha * r_ref[...] + _dot(a_ref[...], wbf_ref[...])


def _matmul(a, w, layer, col_off, n_cols, tm, tn, out_dtype, name, residual=None, alpha=None):
    m, k = a.shape
    tm = min(tm, m)
    while n_cols % tn or col_off % tn:
        tn //= 2
    col_block_off = col_off // tn
    in_specs = [
        pl.BlockSpec((tm, k), lambda n, i: (i, 0)),
        pl.BlockSpec((None, k, tn), lambda n, i: (layer, 0, n + col_block_off)),
    ]
    operands = [a, w]
    body = _mm_kernel
    if residual is not None:
        in_specs.append(pl.BlockSpec((tm, tn), lambda n, i: (i, n)))
        operands.append(residual)
        body = functools.partial(_mm_residual_kernel, alpha=alpha)
    return pl.pallas_call(
        body,
        grid=(n_cols // tn, m // tm),
        in_specs=in_specs,
        out_specs=pl.BlockSpec((tm, tn), lambda n, i: (i, n)),
        out_shape=jax.ShapeDtypeStruct((m, n_cols), out_dtype),
        scratch_shapes=[pltpu.VMEM((k, tn), BF16)],
        compiler_params=_params(("arbitrary", "arbitrary")),
        name=name,
    )(*operands)


def _attn_kernel(q_ref, k_ref, v_ref, o_ref, *, tq, kb, ks, hp, scale):
    qi = pl.program_id(2)
    n_sub = ks // kb
    n_steps = ((qi + 1) * tq + ks - 1) // ks
    t_pos = lax.broadcasted_iota(I32, (tq, kb), 0) + qi * tq
    s_loc = lax.broadcasted_iota(I32, (tq, kb), 1)
    r = lax.broadcasted_iota(I32, (2 * kb, 2 * kb), 0)
    c = lax.broadcasted_iota(I32, (2 * kb, 2 * kb), 1)
    r = jnp.where(r >= kb, r - kb, r)
    suffix = jnp.where((c >= kb) | (r > c), 1.0, 0.0).astype(BF16)

    def step(j, carry, masked):
        k0 = pl.multiple_of(j * ks, ks)
        return tuple(head_step(h, k0, carry[h], masked) for h in range(hp))

    def head_step(h, k0, carry, masked):
        acc, later_sum = carry
        cols = slice(h * HEAD_DIM, (h + 1) * HEAD_DIM)
        q = q_ref[:, cols]
        kj = k_ref[pl.ds(k0, ks), cols]
        vj = v_ref[pl.ds(k0, ks), cols]
        z = lax.dot_general(q, kj, (((1,), (1,)), ((), ())), preferred_element_type=F32) * (scale * LOG2_E)
        weights = [None] * n_sub
        for s in reversed(range(n_sub)):
            zs = z[:, s * kb:(s + 1) * kb]
            softplus = jnp.log(1.0 + jnp.exp2(jnp.minimum(zs, -zs))) * LOG2_E
            log_beta = jnp.minimum(zs, 0.0) - softplus
            log_one_minus = log_beta - zs
            if masked:
                strict = (s_loc + (k0 + s * kb)) < t_pos
                log_one_minus = jnp.where(strict, log_one_minus, 0.0)
            hi = log_one_minus.astype(BF16)
            lo = (log_one_minus - hi.astype(F32)).astype(BF16)
            sums = _dot(jnp.concatenate([hi, lo], axis=1), suffix)
            a = jnp.exp2(log_beta + (sums[:, :kb] + later_sum))
            if masked:
                a = jnp.where(strict, a, 0.0)
            weights[s] = a.astype(BF16)
            later_sum = later_sum + sums[:, kb:]
        acc = acc + _dot(jnp.concatenate(weights, axis=1), vj)
        return acc, later_sum

    carry = tuple((jnp.zeros((tq, HEAD_DIM), F32), jnp.zeros((tq, kb), F32)) for _ in range(hp))
    carry = step(n_steps - 1, carry, True)
    carry = lax.fori_loop(0, n_steps - 1, lambda t, cr: step(n_steps - 2 - t, cr, False), carry)
    for h in range(hp):
        o_ref[:, h * HEAD_DIM:(h + 1) * HEAD_DIM] = carry[h][0].astype(o_ref.dtype)


def _attention(qkv, batch, seq, heads, tq, ks, hp):
    m = qkv.shape[0]
    tq = min(tq, seq)
    ks = min(ks, seq)
    hp = min(hp, heads)
    assert ks % tq == 0 and heads % hp == 0
    kb = HEAD_DIM
    nq = seq // tq
    groups = heads // hp
    width = hp * HEAD_DIM
    kern = functools.partial(_attn_kernel, tq=tq, kb=kb, ks=ks, hp=hp, scale=HEAD_DIM ** -0.5)
    return pl.pallas_call(
        kern,
        grid=(batch, groups, nq),
        in_specs=[
            pl.BlockSpec((tq, width), lambda b, g, i: (b * nq + i, g)),
            pl.BlockSpec((seq, width), lambda b, g, i: (b, groups + g)),
            pl.BlockSpec((seq, width), lambda b, g, i: (b, 2 * groups + g)),
        ],
        out_specs=pl.BlockSpec((tq, width), lambda b, g, i: (b * nq + i, g)),
        out_shape=jax.ShapeDtypeStruct((m, heads * HEAD_DIM), BF16),
        compiler_params=_params(("arbitrary", "arbitrary", "arbitrary")),
        name="sb_attention",
    )(qkv, qkv, qkv)


def _gelu_tanh(x):
    return 0.5 * x * (1.0 + jnp.tanh(0.7978845608028654 * (x + 0.044715 * (x * x * x))))


def _gmlp_conv_kernel(gu_ref, gv_ref, cb_ref, cc_ref, cx_ref, ccp_ref, cxp_ref, lng_ref, lnb_ref,
                      wsp_ref, bsp_ref, cw_ref, o_ref, *, tt, seq, groups):
    i = pl.program_id(0)
    z = cc_ref[...] * cx_ref[...]
    first = (i * tt) % seq == 0
    zp = jnp.where(first, 0.0, ccp_ref[...] * cxp_ref[...])
    row = lax.broadcasted_iota(I32, z.shape, 0)
    z1 = jnp.where(row == 0, zp[SUBLANES - 1:SUBLANES, :], pltpu.roll(z, 1, 0))
    z2 = pltpu.roll(z, 2, 0)
    z2 = jnp.where(row == 0, zp[SUBLANES - 2:SUBLANES - 1, :], z2)
    z2 = jnp.where(row == 1, zp[SUBLANES - 1:SUBLANES, :], z2)
    y = cw_ref[0:1, :] * z2 + cw_ref[1:2, :] * z1 + cw_ref[2:3, :] * z
    o_ref[1] = (cb_ref[...] * y).astype(o_ref.dtype)

    v = _gelu_tanh(gv_ref[...])
    mu = jnp.mean(v, axis=-1, keepdims=True)
    var = jnp.mean(jnp.square(v - mu), axis=-1, keepdims=True)
    vn = ((v - mu) * lax.rsqrt(var + LN_EPS) * lng_ref[...] + lnb_ref[...]).astype(BF16)
    u = _gelu_tanh(gu_ref[...])
    tr = lax.broadcasted_iota(I32, (GM_SPAN, GM_SPAN), 0) // CHUNK
    sc = lax.broadcasted_iota(I32, (GM_SPAN, GM_SPAN), 1) // CHUNK
    causal = tr >= sc
    for g in range(groups):
        wm = jnp.where(causal, wsp_ref[g], 0.0).astype(BF16)
        cols = slice(g * GM_GROUP_DIM, (g + 1) * GM_GROUP_DIM)
        for s in range(tt // GM_SPAN):
            rows = slice(s * GM_SPAN, (s + 1) * GM_SPAN)
            f = _dot(wm, vn[rows, cols]) + bsp_ref[g]
            o_ref[0, rows, cols] = (u[rows, cols] * f).astype(o_ref.dtype)


def _gmlp_conv(rest, gm_ln_g, gm_ln_b, w_spatial, b_spatial_bcast, conv_w, layer, seq, bw, tt):
    m = rest.shape[0]
    tt = min(tt, seq)
    groups = bw // GM_GROUP_DIM
    halo_blocks = tt // SUBLANES
    kern = functools.partial(_gmlp_conv_kernel, tt=tt, seq=seq, groups=groups)

    def part(p):
        return pl.BlockSpec((tt, bw), lambda i: (i, p))

    def halo(p):
        return pl.BlockSpec((SUBLANES, bw), lambda i: (jnp.maximum(i * halo_blocks - 1, 0), p))

    return pl.pallas_call(
        kern,
        grid=(m // tt,),
        in_specs=[
            part(0), part(1), part(2), part(3), part(4), halo(3), halo(4),
            pl.BlockSpec((None, 1, bw), lambda i: (layer, 0, 0)),
            pl.BlockSpec((None, 1, bw), lambda i: (layer, 0, 0)),
            pl.BlockSpec((None, groups, GM_SPAN, GM_SPAN), lambda i: (layer, 0, 0, 0)),
            pl.BlockSpec((None, groups, GM_SPAN, GM_SPAN), lambda i: (layer, 0, 0, 0)),
            pl.BlockSpec((None, 3, bw), lambda i: (layer, 0, 0)),
        ],
        out_specs=pl.BlockSpec((2, tt, bw), lambda i: (0, i, 0)),
        out_shape=jax.ShapeDtypeStruct((2, m, bw), BF16),
        compiler_params=_params(("arbitrary",)),
        name="gmlp_conv",
    )(rest, rest, rest, rest, rest, rest, rest, gm_ln_g, gm_ln_b, w_spatial, b_spatial_bcast, conv_w)


def _merge_kernel(h_ref, oa_ref, obc_ref, wg_ref, bg_ref, wb_ref, o_ref, wg_bf, wb_bf):
    @pl.when(pl.program_id(1) == 0)
    def _():
        for i in range(N_BRANCH):
            _cast_rows_to_bf16(wg_ref.at[i], wg_bf.at[i])
            _cast_rows_to_bf16(wb_ref.at[i], wb_bf.at[i])

    h = h_ref[...]
    merged = None
    for i in range(N_BRANCH):
        gate = jax.nn.sigmoid(_dot(h, wg_bf[i]) + bg_ref[i])
        branch = oa_ref[...] if i == 0 else obc_ref[i - 1]
        term = gate * _dot(branch, wb_bf[i])
        merged = term if merged is None else merged + term
    o_ref[...] = merged.astype(o_ref.dtype)


def _merge(h_bf, o_a, o_bc, w_gate, b_gate4, w_branch, layer, tm, tn):
    m, d = h_bf.shape
    bw = o_a.shape[1]
    tm = min(tm, m)
    tn = min(tn, d)
    return pl.pallas_call(
        _merge_kernel,
        grid=(d // tn, m // tm),
        in_specs=[
            pl.BlockSpec((tm, d), lambda n, i: (i, 0)),
            pl.BlockSpec((tm, bw), lambda n, i: (i, 0)),
            pl.BlockSpec((2, tm, bw), lambda n, i: (0, i, 0)),
            pl.BlockSpec((None, N_BRANCH, d, tn), lambda n, i: (layer, 0, 0, n)),
            pl.BlockSpec((None, N_BRANCH, 1, tn), lambda n, i: (layer, 0, 0, n)),
            pl.BlockSpec((None, N_BRANCH, bw, tn), lambda n, i: (layer, 0, 0, n),
                         pipeline_mode=pl.Buffered(1)),
        ],
        out_specs=pl.BlockSpec((tm, tn), lambda n, i: (i, n)),
        out_shape=jax.ShapeDtypeStruct((m, d), BF16),
        scratch_shapes=[pltpu.VMEM((N_BRANCH, d, tn), BF16), pltpu.VMEM((N_BRANCH, bw, tn), BF16)],
        compiler_params=_params(("arbitrary", "arbitrary")),
        name="branch_merge",
    )(h_bf, o_a, o_bc, w_gate, b_gate4, w_branch)


def _layer_norm_rows(y, g, b):
    mu = jnp.mean(y, axis=-1, keepdims=True)
    var = jnp.mean(jnp.square(y - mu), axis=-1, keepdims=True)
    return (y - mu) * lax.rsqrt(var + LN_EPS) * g + b


def _residual_ln_kernel(y_ref, g_ref, b_ref, o_ref, obf_ref, opk_ref):
    out = _layer_norm_rows(y_ref[...], g_ref[...], b_ref[...])
    o_ref[...] = out
    obf_ref[...] = out.astype(BF16)
    _store_packed_rows(out, opk_ref)


def _residual_ln(y, g, b, layer, tm):
    m, d = y.shape
    tm = min(tm, m)
    row = pl.BlockSpec((tm, d), lambda i: (i, 0))
    vec = pl.BlockSpec((None, 1, d), lambda i: (layer, 0, 0))
    packs = d // PACK_COLS
    packed = pl.BlockSpec((tm * packs, LANES), lambda i: (i, 0))
    return pl.pallas_call(
        _residual_ln_kernel,
        grid=(m // tm,),
        in_specs=[row, vec, vec],
        out_specs=[row, row, packed],
        out_shape=[jax.ShapeDtypeStruct((m, d), F32), jax.ShapeDtypeStruct((m, d), BF16),
                   jax.ShapeDtypeStruct((m * packs, LANES), I32)],
        compiler_params=_params(("arbitrary",)),
        name="residual_ln",
    )(y, g, b)


def _router_kernel(h_ref, w_ref, b_ref, idx_ref, rank_ref, wts_ref, cnt_ref, run_ref, *, tr, n_exp):
    @pl.when(pl.program_id(0) == 0)
    def _():
        run_ref[...] = jnp.zeros_like(run_ref)

    h = h_ref[...]
    w = w_ref[...]
    h_hi = h.astype(BF16)
    h_lo = (h - h_hi.astype(F32)).astype(BF16)
    w_hi = w.astype(BF16)
    w_lo = (w - w_hi.astype(F32)).astype(BF16)
    logits = _dot(h_hi, w_hi) + (_dot(h_hi, w_lo) + _dot(h_lo, w_hi))
    scores = jax.nn.sigmoid(logits)
    sel = scores + b_ref[...]
    lane = lax.broadcasted_iota(I32, (tr, n_exp), 1).astype(F32)
    out_lane = lax.broadcasted_iota(I32, (tr, IDX_LANES), 1)
    mask = jnp.zeros((tr, n_exp), F32)
    idx_out = jnp.zeros((tr, IDX_LANES), F32)
    picks = []
    for k in range(TOP_K):
        best = jnp.max(sel, axis=-1, keepdims=True)
        pick = jnp.min(jnp.where(sel == best, lane, float(n_exp)), axis=-1, keepdims=True)
        hit = lane == pick
        mask = jnp.where(hit, 1.0, mask)
        sel = jnp.where(hit, -jnp.inf, sel)
        idx_out = jnp.where(out_lane == k, pick, idx_out)
        picks.append(hit)
    chosen_scores = mask * scores
    gates = chosen_scores / jnp.sum(chosen_scores, axis=-1, keepdims=True) * ROUTED_SCALE

    rr = lax.broadcasted_iota(I32, (tr, tr), 0)
    cc = lax.broadcasted_iota(I32, (tr, tr), 1)
    lower = jnp.where(rr > cc, 1.0, 0.0).astype(BF16)
    rank = _dot(lower, mask.astype(BF16)) + run_ref[...]
    run_ref[...] = run_ref[...] + jnp.sum(mask, axis=0, keepdims=True)
    cnt_ref[...] = run_ref[...].astype(I32)

    rank_out = jnp.zeros((tr, IDX_LANES), F32)
    wts_out = jnp.zeros((tr, IDX_LANES), F32)
    for k in range(TOP_K):
        rank_out = jnp.where(out_lane == k, jnp.sum(jnp.where(picks[k], rank, 0.0), axis=-1, keepdims=True), rank_out)
        wts_out = jnp.where(out_lane == k, jnp.sum(jnp.where(picks[k], gates, 0.0), axis=-1, keepdims=True), wts_out)
    idx_ref[...] = idx_out.astype(I32)
    rank_ref[...] = rank_out.astype(I32)
    wts_ref[...] = wts_out


def _router(h, w_router, b_router3, layer, tr):
    m, d = h.shape
    n_exp = w_router.shape[-1]
    tr = min(tr, m)
    lists = pl.BlockSpec((tr, IDX_LANES), lambda i: (i, 0))
    return pl.pallas_call(
        functools.partial(_router_kernel, tr=tr, n_exp=n_exp),
        grid=(m // tr,),
        in_specs=[
            pl.BlockSpec((tr, d), lambda i: (i, 0)),
            pl.BlockSpec((None, d, n_exp), lambda i: (layer, 0, 0)),
            pl.BlockSpec((None, 1, n_exp), lambda i: (layer, 0, 0)),
        ],
        out_specs=[lists, lists, lists, pl.BlockSpec((1, n_exp), lambda i: (0, 0))],
        out_shape=[
            jax.ShapeDtypeStruct((m, IDX_LANES), I32),
            jax.ShapeDtypeStruct((m, IDX_LANES), I32),
            jax.ShapeDtypeStruct((m, IDX_LANES), F32),
            jax.ShapeDtypeStruct((1, n_exp), I32),
        ],
        scratch_shapes=[pltpu.VMEM((1, n_exp), F32)],
        compiler_params=_params(("arbitrary",)),
        name="router_topk",
    )(h, w_router, b_router3)


def _inverse_kernel(dest_ref, zeros_hbm, src_ref, sem, *, n_pairs):
    clear = pltpu.make_async_copy(zeros_hbm, src_ref, sem)
    clear.start()
    clear.wait()

    def place(t, c):
        for k in range(TOP_K):
            src_ref[dest_ref[t * TOP_K + k]] = t
        return c

    lax.fori_loop(0, n_pairs // TOP_K, place, 0, unroll=2)


def _inverse_map(dest_flat, n_rows):
    n_pairs = dest_flat.shape[0]
    smem = pl.BlockSpec(memory_space=pltpu.SMEM)
    return pl.pallas_call(
        functools.partial(_inverse_kernel, n_pairs=n_pairs),
        in_specs=[smem, pl.BlockSpec(memory_space=pl.ANY)],
        out_specs=smem,
        out_shape=jax.ShapeDtypeStruct((n_rows,), I32),
        scratch_shapes=[pltpu.SemaphoreType.DMA(())],
        name="inverse_map",
    )(dest_flat, jnp.zeros((n_rows,), I32))


def _route_plan_kernel(cnt_ref, idx_ref, rank_ref, dest_ref, te_ref, nu_ref, nxt_ref, off_ref, *,
                       n_exp, tile, n_tiles):
    shift = tile.bit_length() - 1

    def link(t, following):
        e = n_exp - 1 - t
        nxt_ref[e] = following
        return jnp.where(cnt_ref[e] > 0, e, following)

    lax.fori_loop(0, n_exp, link, jnp.int32(-1))

    def per_expert(e, carry):
        tiles_before, last_owner = carry
        nt = lax.shift_right_logical(cnt_ref[e] + (tile - 1), shift)
        off_ref[e] = tiles_before * tile

        def fill(q, c):
            te_ref[tiles_before + q] = e
            return c

        lax.fori_loop(0, nt, fill, 0)
        return tiles_before + nt, jnp.where(nt > 0, e, last_owner)

    used, last_owner = lax.fori_loop(0, n_exp, per_expert, (jnp.int32(0), jnp.int32(0)))
    nu_ref[0] = used

    def fill_tail(q, c):
        te_ref[q] = last_owner
        return c

    lax.fori_loop(used, n_tiles, fill_tail, 0)

    idx = idx_ref[...]
    offset = lax.fori_loop(0, n_exp, lambda e, acc: jnp.where(idx == e, off_ref[e], acc),
                           jnp.zeros_like(idx))
    dest_ref[...] = offset + rank_ref[...]


def _route_plan(counts, idx, rank, tile, n_tiles):
    n_exp = counts.shape[-1]
    pairs = idx[:, :TOP_K].reshape(-1, LANES)
    ranks = rank[:, :TOP_K].reshape(-1, LANES)
    smem = pl.BlockSpec(memory_space=pltpu.SMEM)
    vmem = pl.BlockSpec(memory_space=pltpu.VMEM)
    assert tile & (tile - 1) == 0
    return pl.pallas_call(
        functools.partial(_route_plan_kernel, n_exp=n_exp, tile=tile, n_tiles=n_tiles),
        in_specs=[smem, vmem, vmem],
        out_specs=[vmem, smem, smem, smem],
        out_shape=[jax.ShapeDtypeStruct(pairs.shape, I32), jax.ShapeDtypeStruct((n_tiles,), I32),
                   jax.ShapeDtypeStruct((1,), I32), jax.ShapeDtypeStruct((n_exp,), I32)],
        scratch_shapes=[pltpu.SMEM((n_exp,), I32)],
        name="route_plan",
    )(counts.reshape(n_exp), pairs, ranks)


def _expert_kernel(te_ref, nu_ref, src_ref, nxt_ref, h_hbm, wg_hbm, wu_hbm, wd_hbm, o_ref,
                   xbuf, x_bf, wg_f32, wu_f32, wd_f32, wg_bf, wu_bf, wd_bf, wslot, sem, wsem,
                   *, tile, layer):
    i = pl.program_id(0)
    n_used = nu_ref[0]
    slot = lax.rem(i, GATHER_SLOTS)

    packs = x_bf.shape[1] // PACK_COLS
    pitch = xbuf.shape[0] // (GATHER_SLOTS * tile)

    def weight_copies(expert, s):
        return [pltpu.make_async_copy(hbm.at[layer, expert], buf.at[s], wsem.at[s, n])
                for n, (hbm, buf) in enumerate(((wg_hbm, wg_f32), (wu_hbm, wu_f32), (wd_hbm, wd_f32)))]

    def start_row(tile_idx, s, r):
        src_row = pl.multiple_of(src_ref[tile_idx * tile + r] * packs, packs)
        dst_row = pl.multiple_of((s * tile + r) * pitch, SUBLANES)
        pltpu.make_async_copy(h_hbm.at[pl.ds(src_row, packs)], xbuf.at[pl.ds(dst_row, packs)],
                              sem.at[s]).start()

    def wait_gather(s):
        dst_row = pl.multiple_of(s * (tile * pitch), SUBLANES)
        pltpu.make_async_copy(h_hbm.at[pl.ds(0, tile * packs)], xbuf.at[pl.ds(dst_row, tile * packs)],
                              sem.at[s]).wait()

    @pl.when(i == 0)
    def _():
        wslot[0] = 0
        for cp in weight_copies(te_ref[0], 0):
            cp.start(priority=1)

        for first in range(GATHER_SLOTS - 1):
            @pl.when(first < n_used)
            def _():
                def body(r, c):
                    start_row(first, first, r)
                    return c
                lax.fori_loop(0, tile, body, 0, unroll=8)

    def compute(prefetch):
        expert = te_ref[i]
        changed = jnp.logical_or(i == 0, expert != te_ref[jnp.maximum(i - 1, 0)])

        @pl.when(changed)
        def _():
            s = wslot[0]
            for cp in weight_copies(expert, s):
                cp.wait()
            following = nxt_ref[expert]

            @pl.when(following >= 0)
            def _():
                for cp in weight_copies(following, 1 - s):
                    cp.start(priority=1)

            _cast_rows_to_bf16(wg_f32.at[s], wg_bf)
            _cast_rows_to_bf16(wu_f32.at[s], wu_bf)
            _cast_rows_to_bf16(wd_f32.at[s], wd_bf)
            wslot[0] = 1 - s

        wait_gather(slot)
        rows_per_chunk = tile // packs
        ahead = GATHER_SLOTS - 1
        ahead_slot = lax.rem(i + ahead, GATHER_SLOTS)
        for j in range(packs):
            if prefetch:
                for r in range(j * rows_per_chunk, (j + 1) * rows_per_chunk):
                    start_row(i + ahead, ahead_slot, r)
            lo, hi = _unpack_bf16_pairs(xbuf[pl.ds(slot * (tile * pitch) + j, tile, stride=pitch), :])
            c0 = j * PACK_COLS
            x_bf[:, c0:c0 + LANES] = lo.astype(BF16)
            x_bf[:, c0 + LANES:c0 + PACK_COLS] = hi.astype(BF16)
        x = x_bf[...]
        act = jax.nn.silu(_dot(x, wg_bf[...])) * _dot(x, wu_bf[...])
        _store_packed_rows(_dot(act.astype(BF16), wd_bf[...]), o_ref)

    @pl.when(i + (GATHER_SLOTS - 1) < n_used)
    def _():
        compute(True)

    @pl.when(jnp.logical_and(i < n_used, i + (GATHER_SLOTS - 1) >= n_used))
    def _():
        compute(False)

    @pl.when(i >= n_used)
    def _():
        o_ref[...] = jnp.zeros_like(o_ref)


def _experts(tile_expert, n_used, src, next_used, h_pk, w_e_gate, w_e_up, w_e_down, layer, n_tiles, tile):
    d, ff = w_e_gate.shape[-2:]
    packs = d // PACK_COLS
    hbm = pl.BlockSpec(memory_space=pl.ANY)

    grid_spec = pltpu.PrefetchScalarGridSpec(
        num_scalar_prefetch=4,
        grid=(n_tiles,),
        in_specs=[hbm, hbm, hbm, hbm],
        out_specs=pl.BlockSpec((tile * packs, LANES), lambda i, te, nu, s, nx: (i, 0)),
        scratch_shapes=[
            pltpu.VMEM((GATHER_SLOTS * tile * _gather_pitch(packs), LANES), I32),
            pltpu.VMEM((tile, d), BF16),
            pltpu.VMEM((2, d, ff), F32),
            pltpu.VMEM((2, d, ff), F32),
            pltpu.VMEM((2, ff, d), F32),
            pltpu.VMEM((d, ff), BF16),
            pltpu.VMEM((d, ff), BF16),
            pltpu.VMEM((ff, d), BF16),
            pltpu.SMEM((1,), I32),
            pltpu.SemaphoreType.DMA((GATHER_SLOTS,)),
            pltpu.SemaphoreType.DMA((2, 3)),
        ],
    )
    return pl.pallas_call(
        functools.partial(_expert_kernel, tile=tile, layer=layer),
        grid_spec=grid_spec,
        out_shape=jax.ShapeDtypeStruct((n_tiles * tile * packs, LANES), I32),
        compiler_params=_params(("arbitrary",)),
        name="expert_mlp",
    )(tile_expert, n_used, src, next_used, h_pk, w_e_gate, w_e_up, w_e_down)


def _swiglu_kernel(a_ref, wg_ref, wu_ref, o_ref, wg_bf, wu_bf):
    @pl.when(pl.program_id(1) == 0)
    def _():
        _cast_rows_to_bf16(wg_ref, wg_bf)
        _cast_rows_to_bf16(wu_ref, wu_bf)

    a = a_ref[...]
    o_ref[...] = (jax.nn.silu(_dot(a, wg_bf[...])) * _dot(a, wu_bf[...])).astype(o_ref.dtype)


def _swiglu(a, w_gate, w_up, layer, tm, tn):
    m, k = a.shape
    n = w_gate.shape[-1]
    tm = min(tm, m)
    tn = min(tn, n)
    wspec = pl.BlockSpec((None, k, tn), lambda j, i: (layer, 0, j))
    return pl.pallas_call(
        _swiglu_kernel,
        grid=(n // tn, m // tm),
        in_specs=[pl.BlockSpec((tm, k), lambda j, i: (i, 0)), wspec, wspec],
        out_specs=pl.BlockSpec((tm, tn), lambda j, i: (i, j)),
        out_shape=jax.ShapeDtypeStruct((m, n), BF16),
        scratch_shapes=[pltpu.VMEM((k, tn), BF16), pltpu.VMEM((k, tn), BF16)],
        compiler_params=_params(("arbitrary", "arbitrary")),
        name="shared_swiglu",
    )(a, w_gate, w_up)


def _combine_kernel(dest_ref, y_hbm, h_ref, sh_ref, wts_ref, g_ref, b_ref, o_ref, obf_ref,
                    ybuf, routed, sem, *, tc, alpha):
    i = pl.program_id(0)
    n_steps = pl.num_programs(0)
    slot = i % 2

    packs = routed.shape[1] // PACK_COLS
    pitch = ybuf.shape[0] // (2 * TOP_K * tc)

    def start_row(step, s, t, k):
        src_row = pl.multiple_of(dest_ref[(step * tc + t) * TOP_K + k] * packs, packs)
        dst_row = pl.multiple_of(((s * TOP_K + k) * tc + t) * pitch, SUBLANES)
        pltpu.make_async_copy(y_hbm.at[pl.ds(src_row, packs)], ybuf.at[pl.ds(dst_row, packs)],
                              sem.at[s]).start(priority=k % 2)

    def wait_gather(s):
        dst_row = pl.multiple_of(s * (TOP_K * tc * pitch), SUBLANES)
        pltpu.make_async_copy(y_hbm.at[pl.ds(0, TOP_K * tc * packs)],
                              ybuf.at[pl.ds(dst_row, TOP_K * tc * packs)], sem.at[s]).wait()

    @pl.when(i == 0)
    def _():
        def body(t, c):
            for k in range(TOP_K):
                start_row(0, 0, t, k)
            return c
        lax.fori_loop(0, tc, body, 0, unroll=2)

    def compute(prefetch_next):
        wait_gather(slot)
        wts = wts_ref[...]
        gate = [jnp.broadcast_to(wts[:, k:k + 1], (tc, LANES)) for k in range(TOP_K)]
        tokens_per_chunk = tc // packs
        for j in range(packs):
            if prefetch_next:
                for t in range(j * tokens_per_chunk, (j + 1) * tokens_per_chunk):
                    for k in range(TOP_K):
                        start_row(i + 1, 1 - slot, t, k)
            acc_lo = acc_hi = None
            for k in range(TOP_K):
                base = (slot * TOP_K + k) * (tc * pitch) + j
                lo, hi = _unpack_bf16_pairs(ybuf[pl.ds(base, tc, stride=pitch), :])
                acc_lo = gate[k] * lo if acc_lo is None else acc_lo + gate[k] * lo
                acc_hi = gate[k] * hi if acc_hi is None else acc_hi + gate[k] * hi
            c0 = j * PACK_COLS
            routed[:, c0:c0 + LANES] = acc_lo
            routed[:, c0 + LANES:c0 + PACK_COLS] = acc_hi
        y = alpha * h_ref[...] + (routed[...] + sh_ref[...])
        out = _layer_norm_rows(y, g_ref[...], b_ref[...])
        o_ref[...] = out
        obf_ref[...] = out.astype(BF16)

    @pl.when(i + 1 < n_steps)
    def _():
        compute(True)

    @pl.when(i + 1 == n_steps)
    def _():
        compute(False)


def _combine(dest_flat, y_sorted, h, shared, wts, g, b, layer, alpha, tc):
    m, d = h.shape
    tc = min(tc, m)
    row = pl.BlockSpec((tc, d), lambda i, dest: (i, 0))
    vec = pl.BlockSpec((None, 1, d), lambda i, dest: (layer, 0, 0))
    grid_spec = pltpu.PrefetchScalarGridSpec(
        num_scalar_prefetch=1,
        grid=(m // tc,),
        in_specs=[
            pl.BlockSpec(memory_space=pl.ANY),
            row, row,
            pl.BlockSpec((tc, IDX_LANES), lambda i, dest: (i, 0)),
            vec, vec,
        ],
        out_specs=[row, row],
        scratch_shapes=[pltpu.VMEM((2 * TOP_K * tc * _gather_pitch(d // PACK_COLS), LANES), I32),
                        pltpu.VMEM((tc, d), F32), pltpu.SemaphoreType.DMA((2,))],
    )
    return pl.pallas_call(
        functools.partial(_combine_kernel, tc=tc, alpha=alpha),
        grid_spec=grid_spec,
        out_shape=[jax.ShapeDtypeStruct((m, d), F32), jax.ShapeDtypeStruct((m, d), BF16)],
        compiler_params=_params(("arbitrary",)),
        name="moe_combine_ln",
    )(dest_flat, y_sorted, h, shared, wts, g, b)


def kernel(x, w_in, w_gate, b_gate, w_branch, w_out, gm_ln_g, gm_ln_b, w_spatial, b_spatial, conv_w,
           ln1_g, ln1_b, w_router, b_router, w_e_gate, w_e_up, w_e_down, w_s_gate, w_s_up, w_s_down,
           ln2_g, ln2_b):
    batch, seq, d = x.shape
    depth = w_in.shape[0]
    m = batch * seq
    bw = d // 4
    heads = bw // HEAD_DIM
    n_exp = w_router.shape[-1]
    alpha = float((2 * depth) ** 0.25)
    tile = EXPERT_ROW_TILE
    n_tiles = (m * TOP_K + n_exp * (tile - 1)) // tile + 1
    n_rows = n_tiles * tile

    b_gate4 = b_gate.reshape(depth, N_BRANCH, 1, d)
    gm_g3 = gm_ln_g.reshape(depth, 1, bw)
    gm_b3 = gm_ln_b.reshape(depth, 1, bw)
    b_sp_bcast = jnp.broadcast_to(b_spatial[..., None], b_spatial.shape + (GM_SPAN,))
    ln1_g3, ln1_b3 = ln1_g.reshape(depth, 1, d), ln1_b.reshape(depth, 1, d)
    ln2_g3, ln2_b3 = ln2_g.reshape(depth, 1, d), ln2_b.reshape(depth, 1, d)
    b_router3 = b_router.reshape(depth, 1, n_exp)

    h = x.reshape(m, d)
    h_bf = h.astype(BF16)
    for l in range(depth):
        qkv = _matmul(h_bf, w_in, l, 0, 3 * bw, *MATMUL_TILE, BF16, "proj_qkv")
        rest = _matmul(h_bf, w_in, l, 3 * bw, 5 * bw, *MATMUL_TILE, F32, "proj_rest")
        o_a = _attention(qkv, batch, seq, heads, *ATTN_TILE)
        o_bc = _gmlp_conv(rest, gm_g3, gm_b3, w_spatial, b_sp_bcast, conv_w, l, seq, bw, TOKEN_TILE)
        merged = _merge(h_bf, o_a, o_bc, w_gate, b_gate4, w_branch, l, *MERGE_TILE)
        y = _matmul(merged, w_out, l, 0, d, *MATMUL_TILE, F32, "out_proj", residual=h, alpha=alpha)
        h, h_bf, h_pk = _residual_ln(y, ln1_g3, ln1_b3, l, TOKEN_TILE)

        idx, rank, wts, counts = _router(h, w_router, b_router3, l, ROUTER_TILE)
        dest2d, tile_expert, n_used, next_used = _route_plan(counts, idx, rank, tile, n_tiles)
        dest = dest2d.reshape(-1)
        src = _inverse_map(dest, n_rows)
        y_sorted = _experts(tile_expert, n_used, src, next_used, h_pk, w_e_gate, w_e_up, w_e_down, l, n_tiles, tile)
        act_s = _swiglu(h_bf, w_s_gate, w_s_up, l, *SWIGLU_TILE)
        shared = _matmul(act_s, w_s_down, l, 0, d, *SHARED_DOWN_TILE, F32, "shared_down")
        h, h_bf = _combine(dest, y_sorted, h, shared, wts, ln2_g3, ln2_b3, l, alpha, COMBINE_TILE)
    return h.reshape(batch, seq, d)
```

```python
import functools

import jax
import jax.numpy as jnp
from jax import lax
from jax.experimental import pallas as pl
from jax.experimental.pallas import tpu as pltpu

F32 = jnp.float32
BF16 = jnp.bfloat16
I32 = jnp.int32

CHUNK = 64
HEAD_DIM = 128
GM_SPAN = 128
GM_GROUP_DIM = 128
IN_PARTS = 8
N_BRANCH = 3
TOP_K = 6
ROUTED_SCALE = 2.5
LN_EPS = 1e-5
LOG2_E = 1.4426950408889634

V7X_VMEM_LIMIT_BYTES = 56 * 1024 * 1024
LANES = 128
SUBLANES = 8
CAST_ROWS = 256
EXPERT_ROW_TILE = 256
GATHER_SLOTS = 3
IDX_LANES = 8
MATMUL_TILE = (1024, 512)
SHARED_DOWN_TILE = (2048, 1024)
SWIGLU_TILE = (1024, 256)
MERGE_TILE = (512, 256)
ATTN_TILE = (512, 512, 4)
TOKEN_TILE = 256
ROUTER_TILE = 512
COMBINE_TILE = 128


def _params(sem):
    return pltpu.CompilerParams(dimension_semantics=sem, vmem_limit_bytes=V7X_VMEM_LIMIT_BYTES)


def _cast_rows_to_bf16(src_ref, dst_ref):
    k = src_ref.shape[0]
    rows = min(CAST_ROWS, k)

    def body(i, c):
        r = pl.multiple_of(i * rows, rows)
        dst_ref[pl.ds(r, rows), :] = src_ref[pl.ds(r, rows), :].astype(BF16)
        return c

    lax.fori_loop(0, k // rows, body, 0)


def _dot(a, b):
    return jnp.dot(a, b, preferred_element_type=F32)


PACK_COLS = 2 * LANES
HIGH_HALF = -65536


def _pack_bf16_pairs(lo, hi):
    lo_bits = lax.bitcast_convert_type(lo.astype(BF16).astype(F32), I32)
    hi_bits = lax.bitcast_convert_type(hi.astype(BF16).astype(F32), I32)
    return hi_bits | lax.shift_right_logical(lo_bits, 16)


def _unpack_bf16_pairs(word):
    lo = lax.bitcast_convert_type(word << 16, F32)
    hi = lax.bitcast_convert_type(word & HIGH_HALF, F32)
    return lo, hi


def _gather_pitch(packs):
    pitch = -(-packs // SUBLANES) * SUBLANES
    return pitch + SUBLANES if pitch % (2 * SUBLANES) == 0 else pitch


def _store_packed_rows(values, pk_ref):
    rows, cols = values.shape
    packs = cols // PACK_COLS
    for j in range(packs):
        c0 = j * PACK_COLS
        word = _pack_bf16_pairs(values[:, c0:c0 + LANES], values[:, c0 + LANES:c0 + PACK_COLS])
        pk_ref[pl.ds(j, rows, stride=packs), :] = word


def _mm_kernel(a_ref, w_ref, o_ref, wbf_ref):
    @pl.when(pl.program_id(1) == 0)
    def _():
        _cast_rows_to_bf16(w_ref, wbf_ref)

    o_ref[...] = _dot(a_ref[...], wbf_ref[...]).astype(o_ref.dtype)


def _mm_residual_kernel(a_ref, w_ref, r_ref, o_ref, wbf_ref, *, alpha):
    @pl.when(pl.program_id(1) == 0)
    def _():
        _cast_rows_to_bf16(w_ref, wbf_ref)

    o_ref[...] = alpha * r_ref[...] + _dot(a_ref[...], wbf_ref[...])


def _matmul(a, w, layer, col_off, n_cols, tm, tn, out_dtype, name, residual=None, alpha=None):
    m, k = a.shape
    tm = min(tm, m)
    while n_cols % tn or col_off % tn:
        tn //= 2
    col_block_off = col_off // tn
    in_specs = [
        pl.BlockSpec((tm, k), lambda n, i: (i, 0)),
        pl.BlockSpec((None, k, tn), lambda n, i: (layer, 0, n + col_block_off)),
    ]
    operands = [a, w]
    body = _mm_kernel
    if residual is not None:
        in_specs.append(pl.BlockSpec((tm, tn), lambda n, i: (i, n)))
        operands.append(residual)
        body = functools.partial(_mm_residual_kernel, alpha=alpha)
    return pl.pallas_call(
        body,
        grid=(n_cols // tn, m // tm),
        in_specs=in_specs,
        out_specs=pl.BlockSpec((tm, tn), lambda n, i: (i, n)),
        out_shape=jax.ShapeDtypeStruct((m, n_cols), out_dtype),
        scratch_shapes=[pltpu.VMEM((k, tn), BF16)],
        compiler_params=_params(("arbitrary", "arbitrary")),
        name=name,
    )(*operands)


def _attn_kernel(q_ref, k_ref, v_ref, o_ref, *, tq, kb, ks, hp, scale):
    qi = pl.program_id(2)
    n_sub = ks // kb
    n_steps = ((qi + 1) * tq + ks - 1) // ks
    r = lax.broadcasted_iota(I32, (2 * kb, 2 * kb), 0)
    c = lax.broadcasted_iota(I32, (2 * kb, 2 * kb), 1)
    r = jnp.where(r >= kb, r - kb, r)
    suffix = jnp.where((c >= kb) | (r > c), 1.0, 0.0).astype(BF16)

    def step(j, carry, masked):
        k0 = pl.multiple_of(j * ks, ks)
        return tuple(head_step(h, k0, carry[h], masked) for h in range(hp))

    def head_step(h, k0, carry, masked):
        acc, later_sum = carry
        cols = slice(h * HEAD_DIM, (h + 1) * HEAD_DIM)
        q = q_ref[:, cols]
        kj = k_ref[pl.ds(k0, ks), cols]
        vj = v_ref[pl.ds(k0, ks), cols]
        z = lax.dot_general(q, kj, (((1,), (1,)), ((), ())), preferred_element_type=F32) * (scale * LOG2_E)
        weights = [None] * n_sub
        for s in reversed(range(n_sub)):
            r0 = s * kb if (masked and ks == tq) else 0
            zs = z[r0:, s * kb:(s + 1) * kb]
            softplus = jnp.log(1.0 + jnp.exp2(jnp.minimum(zs, -zs))) * LOG2_E
            log_beta = jnp.minimum(zs, 0.0) - softplus
            log_one_minus = log_beta - zs
            if masked:
                rows = tq - r0
                key_pos = lax.broadcasted_iota(I32, (rows, kb), 1) + (k0 + s * kb)
                query_pos = lax.broadcasted_iota(I32, (rows, kb), 0) + (qi * tq + r0)
                strict = key_pos < query_pos
                log_one_minus = jnp.where(strict, log_one_minus, 0.0)
            hi = log_one_minus.astype(BF16)
            lo = (log_one_minus - hi.astype(F32)).astype(BF16)
            sums = _dot(jnp.concatenate([hi, lo], axis=1), suffix)
            a = jnp.exp2(log_beta + (sums[:, :kb] + later_sum[r0:]))
            if masked:
                a = jnp.where(strict, a, 0.0)
            row_sum = sums[:, kb:]
            if r0:
                a = jnp.concatenate([jnp.zeros((r0, kb), F32), a], axis=0)
                row_sum = jnp.concatenate([jnp.zeros((r0, kb), F32), row_sum], axis=0)
            weights[s] = a.astype(BF16)
            later_sum = later_sum + row_sum
        acc = acc + _dot(jnp.concatenate(weights, axis=1), vj)
        return acc, later_sum

    carry = tuple((jnp.zeros((tq, HEAD_DIM), F32), jnp.zeros((tq, kb), F32)) for _ in range(hp))
    carry = step(n_steps - 1, carry, True)
    carry = lax.fori_loop(0, n_steps - 1, lambda t, cr: step(n_steps - 2 - t, cr, False), carry)
    for h in range(hp):
        o_ref[:, h * HEAD_DIM:(h + 1) * HEAD_DIM] = carry[h][0].astype(o_ref.dtype)


def _attention(qkv, batch, seq, heads, tq, ks, hp):
    m = qkv.shape[0]
    tq = min(tq, seq)
    ks = min(ks, seq)
    hp = min(hp, heads)
    assert ks % tq == 0 and heads % hp == 0
    kb = HEAD_DIM
    nq = seq // tq
    groups = heads // hp
    width = hp * HEAD_DIM
    kern = functools.partial(_attn_kernel, tq=tq, kb=kb, ks=ks, hp=hp, scale=HEAD_DIM ** -0.5)
    return pl.pallas_call(
        kern,
        grid=(batch, groups, nq),
        in_specs=[
            pl.BlockSpec((tq, width), lambda b, g, i: (b * nq + i, g)),
            pl.BlockSpec((seq, width), lambda b, g, i: (b, groups + g)),
            pl.BlockSpec((seq, width), lambda b, g, i: (b, 2 * groups + g)),
        ],
        out_specs=pl.BlockSpec((tq, width), lambda b, g, i: (b * nq + i, g)),
        out_shape=jax.ShapeDtypeStruct((m, heads * HEAD_DIM), BF16),
        compiler_params=_params(("arbitrary", "arbitrary", "arbitrary")),
        name="sb_attention",
    )(qkv, qkv, qkv)


def _gelu_tanh(x):
    return 0.5 * x * (1.0 + jnp.tanh(0.7978845608028654 * (x + 0.044715 * (x * x * x))))


def _gmlp_conv_kernel(gu_ref, gv_ref, cb_ref, cc_ref, cx_ref, ccp_ref, cxp_ref, lng_ref, lnb_ref,
                      wsp_ref, bsp_ref, cw_ref, o_ref, *, tt, seq, groups):
    i = pl.program_id(0)
    z = cc_ref[...] * cx_ref[...]
    first = (i * tt) % seq == 0
    zp = jnp.where(first, 0.0, ccp_ref[...] * cxp_ref[...])
    row = lax.broadcasted_iota(I32, z.shape, 0)
    z1 = jnp.where(row == 0, zp[SUBLANES - 1:SUBLANES, :], pltpu.roll(z, 1, 0))
    z2 = pltpu.roll(z, 2, 0)
    z2 = jnp.where(row == 0, zp[SUBLANES - 2:SUBLANES - 1, :], z2)
    z2 = jnp.where(row == 1, zp[SUBLANES - 1:SUBLANES, :], z2)
    y = cw_ref[0:1, :] * z2 + cw_ref[1:2, :] * z1 + cw_ref[2:3, :] * z
    o_ref[1] = (cb_ref[...] * y).astype(o_ref.dtype)

    v = _gelu_tanh(gv_ref[...])
    mu = jnp.mean(v, axis=-1, keepdims=True)
    var = jnp.mean(jnp.square(v - mu), axis=-1, keepdims=True)
    vn = ((v - mu) * lax.rsqrt(var + LN_EPS) * lng_ref[...] + lnb_ref[...]).astype(BF16)
    u = _gelu_tanh(gu_ref[...])
    tr = lax.broadcasted_iota(I32, (GM_SPAN, GM_SPAN), 0) // CHUNK
    sc = lax.broadcasted_iota(I32, (GM_SPAN, GM_SPAN), 1) // CHUNK
    causal = tr >= sc
    for g in range(groups):
        wm = jnp.where(causal, wsp_ref[g], 0.0).astype(BF16)
        cols = slice(g * GM_GROUP_DIM, (g + 1) * GM_GROUP_DIM)
        for s in range(tt // GM_SPAN):
            rows = slice(s * GM_SPAN, (s + 1) * GM_SPAN)
            f = _dot(wm, vn[rows, cols]) + bsp_ref[g]
            o_ref[0, rows, cols] = (u[rows, cols] * f).astype(o_ref.dtype)


def _gmlp_conv(rest, gm_ln_g, gm_ln_b, w_spatial, b_spatial_bcast, conv_w, layer, seq, bw, tt):
    m = rest.shape[0]
    tt = min(tt, seq)
    groups = bw // GM_GROUP_DIM
    halo_blocks = tt // SUBLANES
    kern = functools.partial(_gmlp_conv_kernel, tt=tt, seq=seq, groups=groups)

    def part(p):
        return pl.BlockSpec((tt, bw), lambda i: (i, p))

    def halo(p):
        return pl.BlockSpec((SUBLANES, bw), lambda i: (jnp.maximum(i * halo_blocks - 1, 0), p))

    return pl.pallas_call(
        kern,
        grid=(m // tt,),
        in_specs=[
            part(0), part(1), part(2), part(3), part(4), halo(3), halo(4),
            pl.BlockSpec((None, 1, bw), lambda i: (layer, 0, 0)),
            pl.BlockSpec((None, 1, bw), lambda i: (layer, 0, 0)),
            pl.BlockSpec((None, groups, GM_SPAN, GM_SPAN), lambda i: (layer, 0, 0, 0)),
            pl.BlockSpec((None, groups, GM_SPAN, GM_SPAN), lambda i: (layer, 0, 0, 0)),
            pl.BlockSpec((None, 3, bw), lambda i: (layer, 0, 0)),
        ],
        out_specs=pl.BlockSpec((2, tt, bw), lambda i: (0, i, 0)),
        out_shape=jax.ShapeDtypeStruct((2, m, bw), BF16),
        compiler_params=_params(("arbitrary",)),
        name="gmlp_conv",
    )(rest, rest, rest, rest, rest, rest, rest, gm_ln_g, gm_ln_b, w_spatial, b_spatial_bcast, conv_w)


def _merge_kernel(h_ref, oa_ref, obc_ref, wg_ref, bg_ref, wb_ref, o_ref, wg_bf, wb_bf):
    @pl.when(pl.program_id(1) == 0)
    def _():
        for i in range(N_BRANCH):
            _cast_rows_to_bf16(wg_ref.at[i], wg_bf.at[i])
            _cast_rows_to_bf16(wb_ref.at[i], wb_bf.at[i])

    h = h_ref[...]
    merged = None
    for i in range(N_BRANCH):
        gate = jax.nn.sigmoid(_dot(h, wg_bf[i]) + bg_ref[i])
        branch = oa_ref[...] if i == 0 else obc_ref[i - 1]
        term = gate * _dot(branch, wb_bf[i])
        merged = term if merged is None else merged + term
    o_ref[...] = merged.astype(o_ref.dtype)


def _merge(h_bf, o_a, o_bc, w_gate, b_gate4, w_branch, layer, tm, tn):
    m, d = h_bf.shape
    bw = o_a.shape[1]
    tm = min(tm, m)
    tn = min(tn, d)
    return pl.pallas_call(
        _merge_kernel,
        grid=(d // tn, m // tm),
        in_specs=[
            pl.BlockSpec((tm, d), lambda n, i: (i, 0)),
            pl.BlockSpec((tm, bw), lambda n, i: (i, 0)),
            pl.BlockSpec((2, tm, bw), lambda n, i: (0, i, 0)),
            pl.BlockSpec((None, N_BRANCH, d, tn), lambda n, i: (layer, 0, 0, n)),
            pl.BlockSpec((None, N_BRANCH, 1, tn), lambda n, i: (layer, 0, 0, n)),
            pl.BlockSpec((None, N_BRANCH, bw, tn), lambda n, i: (layer, 0, 0, n),
                         pipeline_mode=pl.Buffered(1)),
        ],
        out_specs=pl.BlockSpec((tm, tn), lambda n, i: (i, n)),
        out_shape=jax.ShapeDtypeStruct((m, d), BF16),
        scratch_shapes=[pltpu.VMEM((N_BRANCH, d, tn), BF16), pltpu.VMEM((N_BRANCH, bw, tn), BF16)],
        compiler_params=_params(("arbitrary", "arbitrary")),
        name="branch_merge",
    )(h_bf, o_a, o_bc, w_gate, b_gate4, w_branch)


def _layer_norm_rows(y, g, b):
    mu = jnp.mean(y, axis=-1, keepdims=True)
    var = jnp.mean(jnp.square(y - mu), axis=-1, keepdims=True)
    return (y - mu) * lax.rsqrt(var + LN_EPS) * g + b


def _residual_ln_kernel(y_ref, g_ref, b_ref, o_ref, obf_ref, opk_ref):
    out = _layer_norm_rows(y_ref[...], g_ref[...], b_ref[...])
    o_ref[...] = out
    obf_ref[...] = out.astype(BF16)
    _store_packed_rows(out, opk_ref)


def _residual_ln(y, g, b, layer, tm):
    m, d = y.shape
    tm = min(tm, m)
    row = pl.BlockSpec((tm, d), lambda i: (i, 0))
    vec = pl.BlockSpec((None, 1, d), lambda i: (layer, 0, 0))
    packs = d // PACK_COLS
    packed = pl.BlockSpec((tm * packs, LANES), lambda i: (i, 0))
    return pl.pallas_call(
        _residual_ln_kernel,
        grid=(m // tm,),
        in_specs=[row, vec, vec],
        out_specs=[row, row, packed],
        out_shape=[jax.ShapeDtypeStruct((m, d), F32), jax.ShapeDtypeStruct((m, d), BF16),
                   jax.ShapeDtypeStruct((m * packs, LANES), I32)],
        compiler_params=_params(("arbitrary",)),
        name="residual_ln",
    )(y, g, b)


def _router_kernel(h_ref, w_ref, b_ref, idx_ref, rank_ref, wts_ref, cnt_ref, run_ref, *, tr, n_exp):
    @pl.when(pl.program_id(0) == 0)
    def _():
        run_ref[...] = jnp.zeros_like(run_ref)

    h = h_ref[...]
    w = w_ref[...]
    h_hi = h.astype(BF16)
    h_lo = (h - h_hi.astype(F32)).astype(BF16)
    w_hi = w.astype(BF16)
    w_lo = (w - w_hi.astype(F32)).astype(BF16)
    logits = _dot(h_hi, w_hi) + (_dot(h_hi, w_lo) + _dot(h_lo, w_hi))
    scores = jax.nn.sigmoid(logits)
    sel = scores + b_ref[...]
    lane = lax.broadcasted_iota(I32, (tr, n_exp), 1).astype(F32)
    out_lane = lax.broadcasted_iota(I32, (tr, IDX_LANES), 1)
    mask = jnp.zeros((tr, n_exp), F32)
    idx_out = jnp.zeros((tr, IDX_LANES), F32)
    picks = []
    for k in range(TOP_K):
        best = jnp.max(sel, axis=-1, keepdims=True)
        pick = jnp.min(jnp.where(sel == best, lane, float(n_exp)), axis=-1, keepdims=True)
        hit = lane == pick
        mask = jnp.where(hit, 1.0, mask)
        sel = jnp.where(hit, -jnp.inf, sel)
        idx_out = jnp.where(out_lane == k, pick, idx_out)
        picks.append(hit)
    chosen_scores = mask * scores
    gates = chosen_scores / jnp.sum(chosen_scores, axis=-1, keepdims=True) * ROUTED_SCALE

    rr = lax.broadcasted_iota(I32, (tr, tr), 0)
    cc = lax.broadcasted_iota(I32, (tr, tr), 1)
    lower = jnp.where(rr > cc, 1.0, 0.0).astype(BF16)
    rank = _dot(lower, mask.astype(BF16)) + run_ref[...]
    run_ref[...] = run_ref[...] + jnp.sum(mask, axis=0, keepdims=True)
    cnt_ref[...] = run_ref[...].astype(I32)

    rank_out = jnp.zeros((tr, IDX_LANES), F32)
    wts_out = jnp.zeros((tr, IDX_LANES), F32)
    for k in range(TOP_K):
        rank_out = jnp.where(out_lane == k, jnp.sum(jnp.where(picks[k], rank, 0.0), axis=-1, keepdims=True), rank_out)
        wts_out = jnp.where(out_lane == k, jnp.sum(jnp.where(picks[k], gates, 0.0), axis=-1, keepdims=True), wts_out)
    idx_ref[...] = idx_out.astype(I32)
    rank_ref[...] = rank_out.astype(I32)
    wts_ref[...] = wts_out


def _router(h, w_router, b_router3, layer, tr):
    m, d = h.shape
    n_exp = w_router.shape[-1]
    tr = min(tr, m)
    lists = pl.BlockSpec((tr, IDX_LANES), lambda i: (i, 0))
    return pl.pallas_call(
        functools.partial(_router_kernel, tr=tr, n_exp=n_exp),
        grid=(m // tr,),
        in_specs=[
            pl.BlockSpec((tr, d), lambda i: (i, 0)),
            pl.BlockSpec((None, d, n_exp), lambda i: (layer, 0, 0)),
            pl.BlockSpec((None, 1, n_exp), lambda i: (layer, 0, 0)),
        ],
        out_specs=[lists, lists, lists, pl.BlockSpec((1, n_exp), lambda i: (0, 0))],
        out_shape=[
            jax.ShapeDtypeStruct((m, IDX_LANES), I32),
            jax.ShapeDtypeStruct((m, IDX_LANES), I32),
            jax.ShapeDtypeStruct((m, IDX_LANES), F32),
            jax.ShapeDtypeStruct((1, n_exp), I32),
        ],
        scratch_shapes=[pltpu.VMEM((1, n_exp), F32)],
        compiler_params=_params(("arbitrary",)),
        name="router_topk",
    )(h, w_router, b_router3)


def _inverse_kernel(dest_ref, zeros_hbm, src_ref, sem, *, n_pairs):
    clear = pltpu.make_async_copy(zeros_hbm, src_ref, sem)
    clear.start()
    clear.wait()

    def place(t, c):
        for k in range(TOP_K):
            src_ref[dest_ref[t * TOP_K + k]] = t
        return c

    lax.fori_loop(0, n_pairs // TOP_K, place, 0, unroll=2)


def _inverse_map(dest_flat, n_rows):
    n_pairs = dest_flat.shape[0]
    smem = pl.BlockSpec(memory_space=pltpu.SMEM)
    return pl.pallas_call(
        functools.partial(_inverse_kernel, n_pairs=n_pairs),
        in_specs=[smem, pl.BlockSpec(memory_space=pl.ANY)],
        out_specs=smem,
        out_shape=jax.ShapeDtypeStruct((n_rows,), I32),
        scratch_shapes=[pltpu.SemaphoreType.DMA(())],
        name="inverse_map",
    )(dest_flat, jnp.zeros((n_rows,), I32))


def _route_plan_kernel(cnt_ref, idx_ref, rank_ref, dest_ref, te_ref, nu_ref, nxt_ref, off_ref, *,
                       n_exp, tile, n_tiles):
    shift = tile.bit_length() - 1

    def link(t, following):
        e = n_exp - 1 - t
        nxt_ref[e] = following
        return jnp.where(cnt_ref[e] > 0, e, following)

    lax.fori_loop(0, n_exp, link, jnp.int32(-1))

    def per_expert(e, carry):
        tiles_before, last_owner = carry
        nt = lax.shift_right_logical(cnt_ref[e] + (tile - 1), shift)
        off_ref[e] = tiles_before * tile

        def fill(q, c):
            te_ref[tiles_before + q] = e
            return c

        lax.fori_loop(0, nt, fill, 0)
        return tiles_before + nt, jnp.where(nt > 0, e, last_owner)

    used, last_owner = lax.fori_loop(0, n_exp, per_expert, (jnp.int32(0), jnp.int32(0)))
    nu_ref[0] = used

    def fill_tail(q, c):
        te_ref[q] = last_owner
        return c

    lax.fori_loop(used, n_tiles, fill_tail, 0)

    idx = idx_ref[...]
    offset = lax.fori_loop(0, n_exp, lambda e, acc: jnp.where(idx == e, off_ref[e], acc),
                           jnp.zeros_like(idx))
    dest_ref[...] = offset + rank_ref[...]


def _route_plan(counts, idx, rank, tile, n_tiles):
    n_exp = counts.shape[-1]
    pairs = idx[:, :TOP_K].reshape(-1, LANES)
    ranks = rank[:, :TOP_K].reshape(-1, LANES)
    smem = pl.BlockSpec(memory_space=pltpu.SMEM)
    vmem = pl.BlockSpec(memory_space=pltpu.VMEM)
    assert tile & (tile - 1) == 0
    return pl.pallas_call(
        functools.partial(_route_plan_kernel, n_exp=n_exp, tile=tile, n_tiles=n_tiles),
        in_specs=[smem, vmem, vmem],
        out_specs=[vmem, smem, smem, smem],
        out_shape=[jax.ShapeDtypeStruct(pairs.shape, I32), jax.ShapeDtypeStruct((n_tiles,), I32),
                   jax.ShapeDtypeStruct((1,), I32), jax.ShapeDtypeStruct((n_exp,), I32)],
        scratch_shapes=[pltpu.SMEM((n_exp,), I32)],
        name="route_plan",
    )(counts.reshape(n_exp), pairs, ranks)


def _expert_kernel(te_ref, nu_ref, src_ref, nxt_ref, h_hbm, wg_hbm, wu_hbm, wd_hbm, o_ref,
                   xbuf, x_bf, wg_f32, wu_f32, wd_f32, wg_bf, wu_bf, wd_bf, wslot, sem, wsem,
                   *, tile, layer):
    i = pl.program_id(0)
    n_used = nu_ref[0]
    slot = lax.rem(i, GATHER_SLOTS)

    packs = x_bf.shape[1] // PACK_COLS
    pitch = xbuf.shape[0] // (GATHER_SLOTS * tile)

    def weight_copies(expert, s):
        return [pltpu.make_async_copy(hbm.at[layer, expert], buf.at[s], wsem.at[s, n])
                for n, (hbm, buf) in enumerate(((wg_hbm, wg_f32), (wu_hbm, wu_f32), (wd_hbm, wd_f32)))]

    def start_row(tile_idx, s, r):
        src_row = pl.multiple_of(src_ref[tile_idx * tile + r] * packs, packs)
        dst_row = pl.multiple_of((s * tile + r) * pitch, SUBLANES)
        pltpu.make_async_copy(h_hbm.at[pl.ds(src_row, packs)], xbuf.at[pl.ds(dst_row, packs)],
                              sem.at[s]).start()

    def wait_gather(s):
        dst_row = pl.multiple_of(s * (tile * pitch), SUBLANES)
        pltpu.make_async_copy(h_hbm.at[pl.ds(0, tile * packs)], xbuf.at[pl.ds(dst_row, tile * packs)],
                              sem.at[s]).wait()

    @pl.when(i == 0)
    def _():
        wslot[0] = 0
        for cp in weight_copies(te_ref[0], 0):
            cp.start(priority=1)

        for first in range(GATHER_SLOTS - 1):
            @pl.when(first < n_used)
            def _():
                def body(r, c):
                    start_row(first, first, r)
                    return c
                lax.fori_loop(0, tile, body, 0, unroll=8)

    def compute(prefetch):
        expert = te_ref[i]
        changed = jnp.logical_or(i == 0, expert != te_ref[jnp.maximum(i - 1, 0)])

        @pl.when(changed)
        def _():
            s = wslot[0]
            for cp in weight_copies(expert, s):
                cp.wait()
            following = nxt_ref[expert]

            @pl.when(following >= 0)
            def _():
                for cp in weight_copies(following, 1 - s):
                    cp.start(priority=1)

            _cast_rows_to_bf16(wg_f32.at[s], wg_bf)
            _cast_rows_to_bf16(wu_f32.at[s], wu_bf)
            _cast_rows_to_bf16(wd_f32.at[s], wd_bf)
            wslot[0] = 1 - s

        wait_gather(slot)
        rows_per_chunk = tile // packs
        ahead = GATHER_SLOTS - 1
        ahead_slot = lax.rem(i + ahead, GATHER_SLOTS)
        for j in range(packs):
            if prefetch:
                for r in range(j * rows_per_chunk, (j + 1) * rows_per_chunk):
                    start_row(i + ahead, ahead_slot, r)
            lo, hi = _unpack_bf16_pairs(xbuf[pl.ds(slot * (tile * pitch) + j, tile, stride=pitch), :])
            c0 = j * PACK_COLS
            x_bf[:, c0:c0 + LANES] = lo.astype(BF16)
            x_bf[:, c0 + LANES:c0 + PACK_COLS] = hi.astype(BF16)
        x = x_bf[...]
        act = jax.nn.silu(_dot(x, wg_bf[...])) * _dot(x, wu_bf[...])
        _store_packed_rows(_dot(act.astype(BF16), wd_bf[...]), o_ref)

    @pl.when(i + (GATHER_SLOTS - 1) < n_used)
    def _():
        compute(True)

    @pl.when(jnp.logical_and(i < n_used, i + (GATHER_SLOTS - 1) >= n_used))
    def _():
        compute(False)

    @pl.when(i >= n_used)
    def _():
        o_ref[...] = jnp.zeros_like(o_ref)


def _experts(tile_expert, n_used, src, next_used, h_pk, w_e_gate, w_e_up, w_e_down, layer, n_tiles, tile):
    d, ff = w_e_gate.shape[-2:]
    packs = d // PACK_COLS
    hbm = pl.BlockSpec(memory_space=pl.ANY)

    grid_spec = pltpu.PrefetchScalarGridSpec(
        num_scalar_prefetch=4,
        grid=(n_tiles,),
        in_specs=[hbm, hbm, hbm, hbm],
        out_specs=pl.BlockSpec((tile * packs, LANES), lambda i, te, nu, s, nx: (i, 0)),
        scratch_shapes=[
            pltpu.VMEM((GATHER_SLOTS * tile * _gather_pitch(packs), LANES), I32),
            pltpu.VMEM((tile, d), BF16),
            pltpu.VMEM((2, d, ff), F32),
            pltpu.VMEM((2, d, ff), F32),
            pltpu.VMEM((2, ff, d), F32),
            pltpu.VMEM((d, ff), BF16),
            pltpu.VMEM((d, ff), BF16),
            pltpu.VMEM((ff, d), BF16),
            pltpu.SMEM((1,), I32),
            pltpu.SemaphoreType.DMA((GATHER_SLOTS,)),
            pltpu.SemaphoreType.DMA((2, 3)),
        ],
    )
    return pl.pallas_call(
        functools.partial(_expert_kernel, tile=tile, layer=layer),
        grid_spec=grid_spec,
        out_shape=jax.ShapeDtypeStruct((n_tiles * tile * packs, LANES), I32),
        compiler_params=_params(("arbitrary",)),
        name="expert_mlp",
    )(tile_expert, n_used, src, next_used, h_pk, w_e_gate, w_e_up, w_e_down)


def _swiglu_kernel(a_ref, wg_ref, wu_ref, o_ref, wg_bf, wu_bf):
    @pl.when(pl.program_id(1) == 0)
    def _():
        _cast_rows_to_bf16(wg_ref, wg_bf)
        _cast_rows_to_bf16(wu_ref, wu_bf)

    a = a_ref[...]
    o_ref[...] = (jax.nn.silu(_dot(a, wg_bf[...])) * _dot(a, wu_bf[...])).astype(o_ref.dtype)


def _swiglu(a, w_gate, w_up, layer, tm, tn):
    m, k = a.shape
    n = w_gate.shape[-1]
    tm = min(tm, m)
    tn = min(tn, n)
    wspec = pl.BlockSpec((None, k, tn), lambda j, i: (layer, 0, j))
    return pl.pallas_call(
        _swiglu_kernel,
        grid=(n // tn, m // tm),
        in_specs=[pl.BlockSpec((tm, k), lambda j, i: (i, 0)), wspec, wspec],
        out_specs=pl.BlockSpec((tm, tn), lambda j, i: (i, j)),
        out_shape=jax.ShapeDtypeStruct((m, n), BF16),
        scratch_shapes=[pltpu.VMEM((k, tn), BF16), pltpu.VMEM((k, tn), BF16)],
        compiler_params=_params(("arbitrary", "arbitrary")),
        name="shared_swiglu",
    )(a, w_gate, w_up)


def _combine_kernel(dest_ref, y_hbm, h_ref, sh_ref, wts_ref, g_ref, b_ref, o_ref, obf_ref,
                    ybuf, routed, sem, *, tc, alpha):
    i = pl.program_id(0)
    n_steps = pl.num_programs(0)
    slot = i % 2

    packs = routed.shape[1] // PACK_COLS
    pitch = ybuf.shape[0] // (2 * TOP_K * tc)

    def start_row(step, s, t, k):
        src_row = pl.multiple_of(dest_ref[(step * tc + t) * TOP_K + k] * packs, packs)
        dst_row = pl.multiple_of(((s * TOP_K + k) * tc + t) * pitch, SUBLANES)
        pltpu.make_async_copy(y_hbm.at[pl.ds(src_row, packs)], ybuf.at[pl.ds(dst_row, packs)],
                              sem.at[s]).start(priority=k % 2)

    def wait_gather(s):
        dst_row = pl.multiple_of(s * (TOP_K * tc * pitch), SUBLANES)
        pltpu.make_async_copy(y_hbm.at[pl.ds(0, TOP_K * tc * packs)],
                              ybuf.at[pl.ds(dst_row, TOP_K * tc * packs)], sem.at[s]).wait()

    @pl.when(i == 0)
    def _():
        def body(t, c):
            for k in range(TOP_K):
                start_row(0, 0, t, k)
            return c
        lax.fori_loop(0, tc, body, 0, unroll=2)

    def compute(prefetch_next):
        wait_gather(slot)
        wts = wts_ref[...]
        gate = [jnp.broadcast_to(wts[:, k:k + 1], (tc, LANES)) for k in range(TOP_K)]
        tokens_per_chunk = tc // packs
        for j in range(packs):
            if prefetch_next:
                for t in range(j * tokens_per_chunk, (j + 1) * tokens_per_chunk):
                    for k in range(TOP_K):
                        start_row(i + 1, 1 - slot, t, k)
            acc_lo = acc_hi = None
            for k in range(TOP_K):
                base = (slot * TOP_K + k) * (tc * pitch) + j
                lo, hi = _unpack_bf16_pairs(ybuf[pl.ds(base, tc, stride=pitch), :])
                acc_lo = gate[k] * lo if acc_lo is None else acc_lo + gate[k] * lo
                acc_hi = gate[k] * hi if acc_hi is None else acc_hi + gate[k] * hi
            c0 = j * PACK_COLS
            routed[:, c0:c0 + LANES] = acc_lo
            routed[:, c0 + LANES:c0 + PACK_COLS] = acc_hi
        y = alpha * h_ref[...] + (routed[...] + sh_ref[...])
        out = _layer_norm_rows(y, g_ref[...], b_ref[...])
        o_ref[...] = out
        obf_ref[...] = out.astype(BF16)

    @pl.when(i + 1 < n_steps)
    def _():
        compute(True)

    @pl.when(i + 1 == n_steps)
    def _():
        compute(False)


def _combine(dest_flat, y_sorted, h, shared, wts, g, b, layer, alpha, tc):
    m, d = h.shape
    tc = min(tc, m)
    row = pl.BlockSpec((tc, d), lambda i, dest: (i, 0))
    vec = pl.BlockSpec((None, 1, d), lambda i, dest: (layer, 0, 0))
    grid_spec = pltpu.PrefetchScalarGridSpec(
        num_scalar_prefetch=1,
        grid=(m // tc,),
        in_specs=[
            pl.BlockSpec(memory_space=pl.ANY),
            row, row,
            pl.BlockSpec((tc, IDX_LANES), lambda i, dest: (i, 0)),
            vec, vec,
        ],
        out_specs=[row, row],
        scratch_shapes=[pltpu.VMEM((2 * TOP_K * tc * _gather_pitch(d // PACK_COLS), LANES), I32),
                        pltpu.VMEM((tc, d), F32), pltpu.SemaphoreType.DMA((2,))],
    )
    return pl.pallas_call(
        functools.partial(_combine_kernel, tc=tc, alpha=alpha),
        grid_spec=grid_spec,
        out_shape=[jax.ShapeDtypeStruct((m, d), F32), jax.ShapeDtypeStruct((m, d), BF16)],
        compiler_params=_params(("arbitrary",)),
        name="moe_combine_ln",
    )(dest_flat, y_sorted, h, shared, wts, g, b)


def kernel(x, w_in, w_gate, b_gate, w_branch, w_out, gm_ln_g, gm_ln_b, w_spatial, b_spatial, conv_w,
           ln1_g, ln1_b, w_router, b_router, w_e_gate, w_e_up, w_e_down, w_s_gate, w_s_up, w_s_down,
           ln2_g, ln2_b):
    batch, seq, d = x.shape
    depth = w_in.shape[0]
    m = batch * seq
    bw = d // 4
    heads = bw // HEAD_DIM
    n_exp = w_router.shape[-1]
    alpha = float((2 * depth) ** 0.25)
    tile = EXPERT_ROW_TILE
    n_tiles = (m * TOP_K + n_exp * (tile - 1)) // tile + 1
    n_rows = n_tiles * tile

    b_gate4 = b_gate.reshape(depth, N_BRANCH, 1, d)
    gm_g3 = gm_ln_g.reshape(depth, 1, bw)
    gm_b3 = gm_ln_b.reshape(depth, 1, bw)
    b_sp_bcast = jnp.broadcast_to(b_spatial[..., None], b_spatial.shape + (GM_SPAN,))
    ln1_g3, ln1_b3 = ln1_g.reshape(depth, 1, d), ln1_b.reshape(depth, 1, d)
    ln2_g3, ln2_b3 = ln2_g.reshape(depth, 1, d), ln2_b.reshape(depth, 1, d)
    b_router3 = b_router.reshape(depth, 1, n_exp)

    h = x.reshape(m, d)
    h_bf = h.astype(BF16)
    for l in range(depth):
        qkv = _matmul(h_bf, w_in, l, 0, 3 * bw, *MATMUL_TILE, BF16, "proj_qkv")
        rest = _matmul(h_bf, w_in, l, 3 * bw, 5 * bw, *MATMUL_TILE, F32, "proj_rest")
        o_a = _attention(qkv, batch, seq, heads, *ATTN_TILE)
        o_bc = _gmlp_conv(rest, gm_g3, gm_b3, w_spatial, b_sp_bcast, conv_w, l, seq, bw, TOKEN_TILE)
        merged = _merge(h_bf, o_a, o_bc, w_gate, b_gate4, w_branch, l, *MERGE_TILE)
        y = _matmul(merged, w_out, l, 0, d, *MATMUL_TILE, F32, "out_proj", residual=h, alpha=alpha)
        h, h_bf, h_pk = _residual_ln(y, ln1_g3, ln1_b3, l, TOKEN_TILE)

        idx, rank, wts, counts = _router(h, w_router, b_router3, l, ROUTER_TILE)
        dest2d, tile_expert, n_used, next_used = _route_plan(counts, idx, rank, tile, n_tiles)
        dest = dest2d.reshape(-1)
        src = _inverse_map(dest, n_rows)
        y_sorted = _experts(tile_expert, n_used, src, next_used, h_pk, w_e_gate, w_e_up, w_e_down, l, n_tiles, tile)
        act_s = _swiglu(h_bf, w_s_gate, w_s_up, l, *SWIGLU_TILE)
        shared = _matmul(act_s, w_s_down, l, 0, d, *SHARED_DOWN_TILE, F32, "shared_down")
        h, h_bf = _combine(dest, y_sorted, h, shared, wts, ln2_g3, ln2_b3, l, alpha, COMBINE_TILE)
    return h.reshape(batch, seq, d)
```

```python
import functools

import jax
import jax.numpy as jnp
from jax import lax
from jax.experimental import pallas as pl
from jax.experimental.pallas import tpu as pltpu

F32 = jnp.float32
BF16 = jnp.bfloat16
I32 = jnp.int32

CHUNK = 64
HEAD_DIM = 128
GM_SPAN = 128
GM_GROUP_DIM = 128
IN_PARTS = 8
N_BRANCH = 3
TOP_K = 6
ROUTED_SCALE = 2.5
LN_EPS = 1e-5
LOG2_E = 1.4426950408889634

V7X_VMEM_LIMIT_BYTES = 56 * 1024 * 1024
LANES = 128
SUBLANES = 8
CAST_ROWS = 256
EXPERT_ROW_TILE = 256
GATHER_SLOTS = 3
IDX_LANES = 8
MATMUL_TILE = (1024, 512)
SHARED_DOWN_TILE = (2048, 1024)
SWIGLU_TILE = (1024, 256)
MERGE_TILE = (512, 256)
ATTN_TILE = (512, 512, 4)
TOKEN_TILE = 256
ROUTER_TILE = 512
COMBINE_TILE = 128


def _params(sem):
    return pltpu.CompilerParams(dimension_semantics=sem, vmem_limit_bytes=V7X_VMEM_LIMIT_BYTES)


def _cast_rows_to_bf16(src_ref, dst_ref):
    k = src_ref.shape[0]
    rows = min(CAST_ROWS, k)

    def body(i, c):
        r = pl.multiple_of(i * rows, rows)
        dst_ref[pl.ds(r, rows), :] = src_ref[pl.ds(r, rows), :].astype(BF16)
        return c

    lax.fori_loop(0, k // rows, body, 0)


def _dot(a, b):
    return jnp.dot(a, b, preferred_element_type=F32)


PACK_COLS = 2 * LANES
HIGH_HALF = -65536


def _pack_bf16_pairs(lo, hi):
    lo_bits = lax.bitcast_convert_type(lo.astype(BF16).astype(F32), I32)
    hi_bits = lax.bitcast_convert_type(hi.astype(BF16).astype(F32), I32)
    return hi_bits | lax.shift_right_logical(lo_bits, 16)


def _unpack_bf16_pairs(word):
    lo = lax.bitcast_convert_type(word << 16, F32)
    hi = lax.bitcast_convert_type(word & HIGH_HALF, F32)
    return lo, hi


def _gather_pitch(packs):
    pitch = -(-packs // SUBLANES) * SUBLANES
    return pitch + SUBLANES if pitch % (2 * SUBLANES) == 0 else pitch


def _store_packed_rows(values, pk_ref):
    rows, cols = values.shape
    packs = cols // PACK_COLS
    for j in range(packs):
        c0 = j * PACK_COLS
        word = _pack_bf16_pairs(values[:, c0:c0 + LANES], values[:, c0 + LANES:c0 + PACK_COLS])
        pk_ref[pl.ds(j, rows, stride=packs), :] = word


def _mm_kernel(a_ref, w_ref, o_ref, wbf_ref):
    @pl.when(pl.program_id(1) == 0)
    def _():
        _cast_rows_to_bf16(w_ref, wbf_ref)

    o_ref[...] = _dot(a_ref[...], wbf_ref[...]).astype(o_ref.dtype)


def _mm_residual_kernel(a_ref, w_ref, r_ref, o_ref, wbf_ref, *, alpha):
    @pl.when(pl.program_id(1) == 0)
    def _():
        _cast_rows_to_bf16(w_ref, wbf_ref)

    o_ref[...] = alpha * r_ref[...] + _dot(a_ref[...], wbf_ref[...])


def _matmul(a, w, layer, col_off, n_cols, tm, tn, out_dtype, name, residual=None, alpha=None):
    m, k = a.shape
    tm = min(tm, m)
    while n_cols % tn or col_off % tn:
        tn //= 2
    col_block_off = col_off // tn
    in_specs = [
        pl.BlockSpec((tm, k), lambda n, i: (i, 0)),
        pl.BlockSpec((None, k, tn), lambda n, i: (layer, 0, n + col_block_off)),
    ]
    operands = [a, w]
    body = _mm_kernel
    if residual is not None:
        in_specs.append(pl.BlockSpec((tm, tn), lambda n, i: (i, n)))
        operands.append(residual)
        body = functools.partial(_mm_residual_kernel, alpha=alpha)
    return pl.pallas_call(
        body,
        grid=(n_cols // tn, m // tm),
        in_specs=in_specs,
        out_specs=pl.BlockSpec((tm, tn), lambda n, i: (i, n)),
        out_shape=jax.ShapeDtypeStruct((m, n_cols), out_dtype),
        scratch_shapes=[pltpu.VMEM((k, tn), BF16)],
        compiler_params=_params(("arbitrary", "arbitrary")),
        name=name,
    )(*operands)


def _attn_kernel(q_ref, k_ref, v_ref, o_ref, *, tq, kb, ks, hp, scale):
    qi = pl.program_id(2)
    n_sub = ks // kb
    n_steps = ((qi + 1) * tq + ks - 1) // ks
    r = lax.broadcasted_iota(I32, (2 * kb, 2 * kb), 0)
    c = lax.broadcasted_iota(I32, (2 * kb, 2 * kb), 1)
    r = jnp.where(r >= kb, r - kb, r)
    suffix = jnp.where((c >= kb) | (r > c), 1.0, 0.0).astype(BF16)

    def step(j, carry, masked):
        k0 = pl.multiple_of(j * ks, ks)
        return tuple(head_step(h, k0, carry[h], masked) for h in range(hp))

    def head_step(h, k0, carry, masked):
        acc, later_sum = carry
        cols = slice(h * HEAD_DIM, (h + 1) * HEAD_DIM)
        q = q_ref[:, cols]
        kj = k_ref[pl.ds(k0, ks), cols]
        vj = v_ref[pl.ds(k0, ks), cols]
        z = lax.dot_general(q, kj, (((1,), (1,)), ((), ())), preferred_element_type=F32) * (scale * LOG2_E)
        weights = [None] * n_sub
        for s in reversed(range(n_sub)):
            r0 = s * kb if (masked and ks == tq) else 0
            zs = z[r0:, s * kb:(s + 1) * kb]
            softplus = jnp.log(1.0 + jnp.exp2(jnp.minimum(zs, -zs))) * LOG2_E
            log_beta = jnp.minimum(zs, 0.0) - softplus
            log_one_minus = log_beta - zs
            if masked:
                rows = tq - r0
                key_pos = lax.broadcasted_iota(I32, (rows, kb), 1) + (k0 + s * kb)
                query_pos = lax.broadcasted_iota(I32, (rows, kb), 0) + (qi * tq + r0)
                strict = key_pos < query_pos
                log_one_minus = jnp.where(strict, log_one_minus, 0.0)
            hi = log_one_minus.astype(BF16)
            lo = (log_one_minus - hi.astype(F32)).astype(BF16)
            sums = _dot(jnp.concatenate([hi, lo], axis=1), suffix)
            a = jnp.exp2(log_beta + (sums[:, :kb] + later_sum[r0:]))
            if masked:
                a = jnp.where(strict, a, 0.0)
            row_sum = sums[:, kb:]
            if r0:
                a = jnp.concatenate([jnp.zeros((r0, kb), F32), a], axis=0)
                row_sum = jnp.concatenate([jnp.zeros((r0, kb), F32), row_sum], axis=0)
            weights[s] = a.astype(BF16)
            later_sum = later_sum + row_sum
        acc = acc + _dot(jnp.concatenate(weights, axis=1), vj)
        return acc, later_sum

    carry = tuple((jnp.zeros((tq, HEAD_DIM), F32), jnp.zeros((tq, kb), F32)) for _ in range(hp))
    carry = step(n_steps - 1, carry, True)
    carry = lax.fori_loop(0, n_steps - 1, lambda t, cr: step(n_steps - 2 - t, cr, False), carry)
    for h in range(hp):
        o_ref[:, h * HEAD_DIM:(h + 1) * HEAD_DIM] = carry[h][0].astype(o_ref.dtype)


def _attention(qkv, batch, seq, heads, tq, ks, hp):
    m = qkv.shape[0]
    tq = min(tq, seq)
    ks = min(ks, seq)
    hp = min(hp, heads)
    assert ks % tq == 0 and heads % hp == 0
    kb = HEAD_DIM
    nq = seq // tq
    groups = heads // hp
    width = hp * HEAD_DIM
    kern = functools.partial(_attn_kernel, tq=tq, kb=kb, ks=ks, hp=hp, scale=HEAD_DIM ** -0.5)
    return pl.pallas_call(
        kern,
        grid=(batch, groups, nq),
        in_specs=[
            pl.BlockSpec((tq, width), lambda b, g, i: (b * nq + i, g)),
            pl.BlockSpec((seq, width), lambda b, g, i: (b, groups + g)),
            pl.BlockSpec((seq, width), lambda b, g, i: (b, 2 * groups + g)),
        ],
        out_specs=pl.BlockSpec((tq, width), lambda b, g, i: (b * nq + i, g)),
        out_shape=jax.ShapeDtypeStruct((m, heads * HEAD_DIM), BF16),
        compiler_params=_params(("arbitrary", "arbitrary", "arbitrary")),
        name="sb_attention",
    )(qkv, qkv, qkv)


def _gelu_tanh(x):
    return 0.5 * x * (1.0 + jnp.tanh(0.7978845608028654 * (x + 0.044715 * (x * x * x))))


def _gmlp_conv_kernel(gu_ref, gv_ref, cb_ref, cc_ref, cx_ref, ccp_ref, cxp_ref, lng_ref, lnb_ref,
                      wsp_ref, bsp_ref, cw_ref, o_ref, *, tt, seq, groups):
    i = pl.program_id(0)
    z = cc_ref[...] * cx_ref[...]
    first = (i * tt) % seq == 0
    zp = jnp.where(first, 0.0, ccp_ref[...] * cxp_ref[...])
    row = lax.broadcasted_iota(I32, z.shape, 0)
    z1 = jnp.where(row == 0, zp[SUBLANES - 1:SUBLANES, :], pltpu.roll(z, 1, 0))
    z2 = pltpu.roll(z, 2, 0)
    z2 = jnp.where(row == 0, zp[SUBLANES - 2:SUBLANES - 1, :], z2)
    z2 = jnp.where(row == 1, zp[SUBLANES - 1:SUBLANES, :], z2)
    y = cw_ref[0:1, :] * z2 + cw_ref[1:2, :] * z1 + cw_ref[2:3, :] * z
    o_ref[1] = (cb_ref[...] * y).astype(o_ref.dtype)

    v = _gelu_tanh(gv_ref[...])
    mu = jnp.mean(v, axis=-1, keepdims=True)
    var = jnp.mean(jnp.square(v - mu), axis=-1, keepdims=True)
    vn = ((v - mu) * lax.rsqrt(var + LN_EPS) * lng_ref[...] + lnb_ref[...]).astype(BF16)
    u = _gelu_tanh(gu_ref[...])
    tr = lax.broadcasted_iota(I32, (GM_SPAN, GM_SPAN), 0) // CHUNK
    sc = lax.broadcasted_iota(I32, (GM_SPAN, GM_SPAN), 1) // CHUNK
    causal = tr >= sc
    for g in range(groups):
        wm = jnp.where(causal, wsp_ref[g], 0.0).astype(BF16)
        cols = slice(g * GM_GROUP_DIM, (g + 1) * GM_GROUP_DIM)
        for s in range(tt // GM_SPAN):
            rows = slice(s * GM_SPAN, (s + 1) * GM_SPAN)
            f = _dot(wm, vn[rows, cols]) + bsp_ref[g]
            o_ref[0, rows, cols] = (u[rows, cols] * f).astype(o_ref.dtype)


def _gmlp_conv(rest, gm_ln_g, gm_ln_b, w_spatial, b_spatial_bcast, conv_w, layer, seq, bw, tt):
    m = rest.shape[0]
    tt = min(tt, seq)
    groups = bw // GM_GROUP_DIM
    halo_blocks = tt // SUBLANES
    kern = functools.partial(_gmlp_conv_kernel, tt=tt, seq=seq, groups=groups)

    def part(p):
        return pl.BlockSpec((tt, bw), lambda i: (i, p))

    def halo(p):
        return pl.BlockSpec((SUBLANES, bw), lambda i: (jnp.maximum(i * halo_blocks - 1, 0), p))

    return pl.pallas_call(
        kern,
        grid=(m // tt,),
        in_specs=[
            part(0), part(1), part(2), part(3), part(4), halo(3), halo(4),
            pl.BlockSpec((None, 1, bw), lambda i: (layer, 0, 0)),
            pl.BlockSpec((None, 1, bw), lambda i: (layer, 0, 0)),
            pl.BlockSpec((None, groups, GM_SPAN, GM_SPAN), lambda i: (layer, 0, 0, 0)),
            pl.BlockSpec((None, groups, GM_SPAN, GM_SPAN), lambda i: (layer, 0, 0, 0)),
            pl.BlockSpec((None, 3, bw), lambda i: (layer, 0, 0)),
        ],
        out_specs=pl.BlockSpec((2, tt, bw), lambda i: (0, i, 0)),
        out_shape=jax.ShapeDtypeStruct((2, m, bw), BF16),
        compiler_params=_params(("arbitrary",)),
        name="gmlp_conv",
    )(rest, rest, rest, rest, rest, rest, rest, gm_ln_g, gm_ln_b, w_spatial, b_spatial_bcast, conv_w)


def _merge_kernel(h_ref, oa_ref, obc_ref, wg_ref, bg_ref, wb_ref, o_ref, wg_bf, wb_bf):
    @pl.when(pl.program_id(1) == 0)
    def _():
        for i in range(N_BRANCH):
            _cast_rows_to_bf16(wg_ref.at[i], wg_bf.at[i])
            _cast_rows_to_bf16(wb_ref.at[i], wb_bf.at[i])

    h = h_ref[...]
    merged = None
    for i in range(N_BRANCH):
        gate = jax.nn.sigmoid(_dot(h, wg_bf[i]) + bg_ref[i])
        branch = oa_ref[...] if i == 0 else obc_ref[i - 1]
        term = gate * _dot(branch, wb_bf[i])
        merged = term if merged is None else merged + term
    o_ref[...] = merged.astype(o_ref.dtype)


def _merge(h_bf, o_a, o_bc, w_gate, b_gate4, w_branch, layer, tm, tn):
    m, d = h_bf.shape
    bw = o_a.shape[1]
    tm = min(tm, m)
    tn = min(tn, d)
    return pl.pallas_call(
        _merge_kernel,
        grid=(d // tn, m // tm),
        in_specs=[
            pl.BlockSpec((tm, d), lambda n, i: (i, 0)),
            pl.BlockSpec((tm, bw), lambda n, i: (i, 0)),
            pl.BlockSpec((2, tm, bw), lambda n, i: (0, i, 0)),
            pl.BlockSpec((None, N_BRANCH, d, tn), lambda n, i: (layer, 0, 0, n)),
            pl.BlockSpec((None, N_BRANCH, 1, tn), lambda n, i: (layer, 0, 0, n)),
            pl.BlockSpec((None, N_BRANCH, bw, tn), lambda n, i: (layer, 0, 0, n),
                         pipeline_mode=pl.Buffered(1)),
        ],
        out_specs=pl.BlockSpec((tm, tn), lambda n, i: (i, n)),
        out_shape=jax.ShapeDtypeStruct((m, d), BF16),
        scratch_shapes=[pltpu.VMEM((N_BRANCH, d, tn), BF16), pltpu.VMEM((N_BRANCH, bw, tn), BF16)],
        compiler_params=_params(("arbitrary", "arbitrary")),
        name="branch_merge",
    )(h_bf, o_a, o_bc, w_gate, b_gate4, w_branch)


def _layer_norm_rows(y, g, b):
    mu = jnp.mean(y, axis=-1, keepdims=True)
    var = jnp.mean(jnp.square(y - mu), axis=-1, keepdims=True)
    return (y - mu) * lax.rsqrt(var + LN_EPS) * g + b


def _residual_ln_kernel(y_ref, g_ref, b_ref, o_ref, obf_ref, opk_ref):
    out = _layer_norm_rows(y_ref[...], g_ref[...], b_ref[...])
    o_ref[...] = out
    obf_ref[...] = out.astype(BF16)
    _store_packed_rows(out, opk_ref)


def _residual_ln(y, g, b, layer, tm):
    m, d = y.shape
    tm = min(tm, m)
    row = pl.BlockSpec((tm, d), lambda i: (i, 0))
    vec = pl.BlockSpec((None, 1, d), lambda i: (layer, 0, 0))
    packs = d // PACK_COLS
    packed = pl.BlockSpec((tm * packs, LANES), lambda i: (i, 0))
    return pl.pallas_call(
        _residual_ln_kernel,
        grid=(m // tm,),
        in_specs=[row, vec, vec],
        out_specs=[row, row, packed],
        out_shape=[jax.ShapeDtypeStruct((m, d), F32), jax.ShapeDtypeStruct((m, d), BF16),
                   jax.ShapeDtypeStruct((m * packs, LANES), I32)],
        compiler_params=_params(("arbitrary",)),
        name="residual_ln",
    )(y, g, b)


def _router_kernel(h_ref, w_ref, b_ref, idx_ref, rank_ref, wts_ref, cnt_ref, run_ref, *, tr, n_exp):
    @pl.when(pl.program_id(0) == 0)
    def _():
        run_ref[...] = jnp.zeros_like(run_ref)

    h = h_ref[...]
    w = w_ref[...]
    h_hi = h.astype(BF16)
    h_lo = (h - h_hi.astype(F32)).astype(BF16)
    w_hi = w.astype(BF16)
    w_lo = (w - w_hi.astype(F32)).astype(BF16)
    logits = _dot(h_hi, w_hi) + (_dot(h_hi, w_lo) + _dot(h_lo, w_hi))
    scores = jax.nn.sigmoid(logits)
    sel = scores + b_ref[...]
    lane = lax.broadcasted_iota(I32, (tr, n_exp), 1).astype(F32)
    out_lane = lax.broadcasted_iota(I32, (tr, IDX_LANES), 1)
    mask = jnp.zeros((tr, n_exp), F32)
    idx_out = jnp.zeros((tr, IDX_LANES), F32)
    picks = []
    for k in range(TOP_K):
        best = jnp.max(sel, axis=-1, keepdims=True)
        pick = jnp.min(jnp.where(sel == best, lane, float(n_exp)), axis=-1, keepdims=True)
        hit = lane == pick
        mask = jnp.where(hit, 1.0, mask)
        sel = jnp.where(hit, -jnp.inf, sel)
        idx_out = jnp.where(out_lane == k, pick, idx_out)
        picks.append(hit)
    chosen_scores = mask * scores
    gates = chosen_scores / jnp.sum(chosen_scores, axis=-1, keepdims=True) * ROUTED_SCALE

    rr = lax.broadcasted_iota(I32, (tr, tr), 0)
    cc = lax.broadcasted_iota(I32, (tr, tr), 1)
    lower = jnp.where(rr > cc, 1.0, 0.0).astype(BF16)
    rank = _dot(lower, mask.astype(BF16)) + run_ref[...]
    run_ref[...] = run_ref[...] + jnp.sum(mask, axis=0, keepdims=True)
    cnt_ref[...] = run_ref[...].astype(I32)

    rank_out = jnp.zeros((tr, IDX_LANES), F32)
    wts_out = jnp.zeros((tr, IDX_LANES), F32)
    for k in range(TOP_K):
        rank_out = jnp.where(out_lane == k, jnp.sum(jnp.where(picks[k], rank, 0.0), axis=-1, keepdims=True), rank_out)
        wts_out = jnp.where(out_lane == k, jnp.sum(jnp.where(picks[k], gates, 0.0), axis=-1, keepdims=True), wts_out)
    idx_ref[...] = idx_out.astype(I32)
    rank_ref[...] = rank_out.astype(I32)
    wts_ref[...] = wts_out


def _router(h, w_router, b_router3, layer, tr):
    m, d = h.shape
    n_exp = w_router.shape[-1]
    tr = min(tr, m)
    lists = pl.BlockSpec((tr, IDX_LANES), lambda i: (i, 0))
    return pl.pallas_call(
        functools.partial(_router_kernel, tr=tr, n_exp=n_exp),
        grid=(m // tr,),
        in_specs=[
            pl.BlockSpec((tr, d), lambda i: (i, 0)),
            pl.BlockSpec((None, d, n_exp), lambda i: (layer, 0, 0)),
            pl.BlockSpec((None, 1, n_exp), lambda i: (layer, 0, 0)),
        ],
        out_specs=[lists, lists, lists, pl.BlockSpec((1, n_exp), lambda i: (0, 0))],
        out_shape=[
            jax.ShapeDtypeStruct((m, IDX_LANES), I32),
            jax.ShapeDtypeStruct((m, IDX_LANES), I32),
            jax.ShapeDtypeStruct((m, IDX_LANES), F32),
            jax.ShapeDtypeStruct((1, n_exp), I32),
        ],
        scratch_shapes=[pltpu.VMEM((1, n_exp), F32)],
        compiler_params=_params(("arbitrary",)),
        name="router_topk",
    )(h, w_router, b_router3)


def _route_plan_kernel(cnt_ref, idx_ref, rank_ref, dest_ref, te_ref, nu_ref, nxt_ref, off_ref, *,
                       n_exp, tile, n_tiles):
    shift = tile.bit_length() - 1

    def link(t, following):
        e = n_exp - 1 - t
        nxt_ref[e] = following
        return jnp.where(cnt_ref[e] > 0, e, following)

    lax.fori_loop(0, n_exp, link, jnp.int32(-1))

    def per_expert(e, carry):
        tiles_before, last_owner = carry
        nt = lax.shift_right_logical(cnt_ref[e] + (tile - 1), shift)
        off_ref[e] = tiles_before * tile

        def fill(q, c):
            te_ref[tiles_before + q] = e
            return c

        lax.fori_loop(0, nt, fill, 0)
        return tiles_before + nt, jnp.where(nt > 0, e, last_owner)

    used, last_owner = lax.fori_loop(0, n_exp, per_expert, (jnp.int32(0), jnp.int32(0)))
    nu_ref[0] = used

    def fill_tail(q, c):
        te_ref[q] = last_owner
        return c

    lax.fori_loop(used, n_tiles, fill_tail, 0)

    idx = idx_ref[...]
    offset = lax.fori_loop(0, n_exp, lambda e, acc: jnp.where(idx == e, off_ref[e], acc),
                           jnp.zeros_like(idx))
    dest_ref[...] = offset + rank_ref[...]


def _route_plan(counts, idx, rank, tile, n_tiles):
    n_exp = counts.shape[-1]
    pairs = idx[:, :TOP_K].reshape(-1, LANES)
    ranks = rank[:, :TOP_K].reshape(-1, LANES)
    smem = pl.BlockSpec(memory_space=pltpu.SMEM)
    vmem = pl.BlockSpec(memory_space=pltpu.VMEM)
    assert tile & (tile - 1) == 0
    return pl.pallas_call(
        functools.partial(_route_plan_kernel, n_exp=n_exp, tile=tile, n_tiles=n_tiles),
        in_specs=[smem, vmem, vmem],
        out_specs=[vmem, smem, smem, smem],
        out_shape=[jax.ShapeDtypeStruct(pairs.shape, I32), jax.ShapeDtypeStruct((n_tiles,), I32),
                   jax.ShapeDtypeStruct((1,), I32), jax.ShapeDtypeStruct((n_exp,), I32)],
        scratch_shapes=[pltpu.SMEM((n_exp,), I32)],
        name="route_plan",
    )(counts.reshape(n_exp), pairs, ranks)


def _expert_kernel(te_ref, nu_ref, src_ref, nxt_ref, h_hbm, wg_hbm, wu_hbm, wd_hbm, o_ref,
                   xbuf, x_bf, wg_f32, wu_f32, wd_f32, wg_bf, wu_bf, wd_bf, wslot, sem, wsem,
                   *, tile, layer):
    i = pl.program_id(0)
    n_used = nu_ref[0]
    slot = lax.rem(i, GATHER_SLOTS)

    packs = x_bf.shape[1] // PACK_COLS
    pitch = xbuf.shape[0] // (GATHER_SLOTS * tile)

    def weight_copies(expert, s):
        return [pltpu.make_async_copy(hbm.at[layer, expert], buf.at[s], wsem.at[s, n])
                for n, (hbm, buf) in enumerate(((wg_hbm, wg_f32), (wu_hbm, wu_f32), (wd_hbm, wd_f32)))]

    def start_row(tile_idx, s, r):
        src_row = pl.multiple_of(src_ref[tile_idx * tile + r] * packs, packs)
        dst_row = pl.multiple_of((s * tile + r) * pitch, SUBLANES)
        pltpu.make_async_copy(h_hbm.at[pl.ds(src_row, packs)], xbuf.at[pl.ds(dst_row, packs)],
                              sem.at[s]).start()

    def wait_gather(s):
        dst_row = pl.multiple_of(s * (tile * pitch), SUBLANES)
        pltpu.make_async_copy(h_hbm.at[pl.ds(0, tile * packs)], xbuf.at[pl.ds(dst_row, tile * packs)],
                              sem.at[s]).wait()

    @pl.when(i == 0)
    def _():
        wslot[0] = 0
        for cp in weight_copies(te_ref[0], 0):
            cp.start(priority=1)

        for first in range(GATHER_SLOTS - 1):
            @pl.when(first < n_used)
            def _():
                def body(r, c):
                    start_row(first, first, r)
                    return c
                lax.fori_loop(0, tile, body, 0, unroll=8)

    def compute(prefetch):
        expert = te_ref[i]
        changed = jnp.logical_or(i == 0, expert != te_ref[jnp.maximum(i - 1, 0)])

        @pl.when(changed)
        def _():
            s = wslot[0]
            for cp in weight_copies(expert, s):
                cp.wait()
            following = nxt_ref[expert]

            @pl.when(following >= 0)
            def _():
                for cp in weight_copies(following, 1 - s):
                    cp.start(priority=1)

            _cast_rows_to_bf16(wg_f32.at[s], wg_bf)
            _cast_rows_to_bf16(wu_f32.at[s], wu_bf)
            _cast_rows_to_bf16(wd_f32.at[s], wd_bf)
            wslot[0] = 1 - s

        wait_gather(slot)
        rows_per_chunk = tile // packs
        ahead = GATHER_SLOTS - 1
        ahead_slot = lax.rem(i + ahead, GATHER_SLOTS)
        for j in range(packs):
            if prefetch:
                for r in range(j * rows_per_chunk, (j + 1) * rows_per_chunk):
                    start_row(i + ahead, ahead_slot, r)
            lo, hi = _unpack_bf16_pairs(xbuf[pl.ds(slot * (tile * pitch) + j, tile, stride=pitch), :])
            c0 = j * PACK_COLS
            x_bf[:, c0:c0 + LANES] = lo.astype(BF16)
            x_bf[:, c0 + LANES:c0 + PACK_COLS] = hi.astype(BF16)
        x = x_bf[...]
        act = jax.nn.silu(_dot(x, wg_bf[...])) * _dot(x, wu_bf[...])
        _store_packed_rows(_dot(act.astype(BF16), wd_bf[...]), o_ref)

    @pl.when(i + (GATHER_SLOTS - 1) < n_used)
    def _():
        compute(True)

    @pl.when(jnp.logical_and(i < n_used, i + (GATHER_SLOTS - 1) >= n_used))
    def _():
        compute(False)

    @pl.when(i >= n_used)
    def _():
        o_ref[...] = jnp.zeros_like(o_ref)


def _experts(tile_expert, n_used, src, next_used, h_pk, w_e_gate, w_e_up, w_e_down, layer, n_tiles, tile):
    d, ff = w_e_gate.shape[-2:]
    packs = d // PACK_COLS
    hbm = pl.BlockSpec(memory_space=pl.ANY)

    grid_spec = pltpu.PrefetchScalarGridSpec(
        num_scalar_prefetch=4,
        grid=(n_tiles,),
        in_specs=[hbm, hbm, hbm, hbm],
        out_specs=pl.BlockSpec((tile * packs, LANES), lambda i, te, nu, s, nx: (i, 0)),
        scratch_shapes=[
            pltpu.VMEM((GATHER_SLOTS * tile * _gather_pitch(packs), LANES), I32),
            pltpu.VMEM((tile, d), BF16),
            pltpu.VMEM((2, d, ff), F32),
            pltpu.VMEM((2, d, ff), F32),
            pltpu.VMEM((2, ff, d), F32),
            pltpu.VMEM((d, ff), BF16),
            pltpu.VMEM((d, ff), BF16),
            pltpu.VMEM((ff, d), BF16),
            pltpu.SMEM((1,), I32),
            pltpu.SemaphoreType.DMA((GATHER_SLOTS,)),
            pltpu.SemaphoreType.DMA((2, 3)),
        ],
    )
    return pl.pallas_call(
        functools.partial(_expert_kernel, tile=tile, layer=layer),
        grid_spec=grid_spec,
        out_shape=jax.ShapeDtypeStruct((n_tiles * tile * packs, LANES), I32),
        compiler_params=_params(("arbitrary",)),
        name="expert_mlp",
    )(tile_expert, n_used, src, next_used, h_pk, w_e_gate, w_e_up, w_e_down)


INVERSE_UNROLL_TOKENS = 256


def _swiglu_kernel(a_ref, wg_ref, wu_ref, dest_ref, zeros_hbm, o_ref, src_ref, wg_bf, wu_bf, sem,
                   *, tokens_per_step):
    step = pl.program_id(0) * pl.num_programs(1) + pl.program_id(1)

    @pl.when(step == 0)
    def _():
        clear = pltpu.make_async_copy(zeros_hbm, src_ref, sem)
        clear.start()
        clear.wait()

    @pl.when(pl.program_id(1) == 0)
    def _():
        _cast_rows_to_bf16(wg_ref, wg_bf)
        _cast_rows_to_bf16(wu_ref, wu_bf)

    chunk = min(tokens_per_step, INVERSE_UNROLL_TOKENS)

    def place(c, carry):
        t0 = step * tokens_per_step + c * chunk
        for t in range(chunk):
            for k in range(TOP_K):
                src_ref[dest_ref[(t0 + t) * TOP_K + k]] = t0 + t
        return carry

    if tokens_per_step == chunk:
        place(0, 0)
    else:
        lax.fori_loop(0, tokens_per_step // chunk, place, 0)

    a = a_ref[...]
    o_ref[...] = (jax.nn.silu(_dot(a, wg_bf[...])) * _dot(a, wu_bf[...])).astype(o_ref.dtype)


def _swiglu_and_inverse_map(a, w_gate, w_up, dest_flat, n_rows, layer, tm, tn):
    m, k = a.shape
    n = w_gate.shape[-1]
    tm = min(tm, m)
    tn = min(tn, n)
    n_steps = (n // tn) * (m // tm)
    assert m % n_steps == 0 and dest_flat.shape[0] == m * TOP_K
    wspec = pl.BlockSpec((None, k, tn), lambda j, i: (layer, 0, j))
    smem = pl.BlockSpec(memory_space=pltpu.SMEM)
    return pl.pallas_call(
        functools.partial(_swiglu_kernel, tokens_per_step=m // n_steps),
        grid=(n // tn, m // tm),
        in_specs=[pl.BlockSpec((tm, k), lambda j, i: (i, 0)), wspec, wspec, smem,
                  pl.BlockSpec(memory_space=pl.ANY)],
        out_specs=[pl.BlockSpec((tm, tn), lambda j, i: (i, j)), smem],
        out_shape=[jax.ShapeDtypeStruct((m, n), BF16), jax.ShapeDtypeStruct((n_rows,), I32)],
        scratch_shapes=[pltpu.VMEM((k, tn), BF16), pltpu.VMEM((k, tn), BF16), pltpu.SemaphoreType.DMA(())],
        compiler_params=_params(("arbitrary", "arbitrary")),
        name="shared_swiglu",
    )(a, w_gate, w_up, dest_flat, jnp.zeros((n_rows,), I32))


def _combine_kernel(dest_ref, y_hbm, h_ref, sh_ref, wts_ref, g_ref, b_ref, o_ref, obf_ref,
                    ybuf, routed, sem, *, tc, alpha):
    i = pl.program_id(0)
    n_steps = pl.num_programs(0)
    slot = i % 2

    packs = routed.shape[1] // PACK_COLS
    pitch = ybuf.shape[0] // (2 * TOP_K * tc)

    def start_row(step, s, t, k):
        src_row = pl.multiple_of(dest_ref[(step * tc + t) * TOP_K + k] * packs, packs)
        dst_row = pl.multiple_of(((s * TOP_K + k) * tc + t) * pitch, SUBLANES)
        pltpu.make_async_copy(y_hbm.at[pl.ds(src_row, packs)], ybuf.at[pl.ds(dst_row, packs)],
                              sem.at[s]).start(priority=k % 2)

    def wait_gather(s):
        dst_row = pl.multiple_of(s * (TOP_K * tc * pitch), SUBLANES)
        pltpu.make_async_copy(y_hbm.at[pl.ds(0, TOP_K * tc * packs)],
                              ybuf.at[pl.ds(dst_row, TOP_K * tc * packs)], sem.at[s]).wait()

    @pl.when(i == 0)
    def _():
        def body(t, c):
            for k in range(TOP_K):
                start_row(0, 0, t, k)
            return c
        lax.fori_loop(0, tc, body, 0, unroll=2)

    def compute(prefetch_next):
        wait_gather(slot)
        wts = wts_ref[...]
        gate = [jnp.broadcast_to(wts[:, k:k + 1], (tc, LANES)) for k in range(TOP_K)]
        tokens_per_chunk = tc // packs
        for j in range(packs):
            if prefetch_next:
                for t in range(j * tokens_per_chunk, (j + 1) * tokens_per_chunk):
                    for k in range(TOP_K):
                        start_row(i + 1, 1 - slot, t, k)
            acc_lo = acc_hi = None
            for k in range(TOP_K):
                base = (slot * TOP_K + k) * (tc * pitch) + j
                lo, hi = _unpack_bf16_pairs(ybuf[pl.ds(base, tc, stride=pitch), :])
                acc_lo = gate[k] * lo if acc_lo is None else acc_lo + gate[k] * lo
                acc_hi = gate[k] * hi if acc_hi is None else acc_hi + gate[k] * hi
            c0 = j * PACK_COLS
            routed[:, c0:c0 + LANES] = acc_lo
            routed[:, c0 + LANES:c0 + PACK_COLS] = acc_hi
        y = alpha * h_ref[...] + (routed[...] + sh_ref[...])
        out = _layer_norm_rows(y, g_ref[...], b_ref[...])
        o_ref[...] = out
        obf_ref[...] = out.astype(BF16)

    @pl.when(i + 1 < n_steps)
    def _():
        compute(True)

    @pl.when(i + 1 == n_steps)
    def _():
        compute(False)


def _combine(dest_flat, y_sorted, h, shared, wts, g, b, layer, alpha, tc):
    m, d = h.shape
    tc = min(tc, m)
    row = pl.BlockSpec((tc, d), lambda i, dest: (i, 0))
    vec = pl.BlockSpec((None, 1, d), lambda i, dest: (layer, 0, 0))
    grid_spec = pltpu.PrefetchScalarGridSpec(
        num_scalar_prefetch=1,
        grid=(m // tc,),
        in_specs=[
            pl.BlockSpec(memory_space=pl.ANY),
            row, row,
            pl.BlockSpec((tc, IDX_LANES), lambda i, dest: (i, 0)),
            vec, vec,
        ],
        out_specs=[row, row],
        scratch_shapes=[pltpu.VMEM((2 * TOP_K * tc * _gather_pitch(d // PACK_COLS), LANES), I32),
                        pltpu.VMEM((tc, d), F32), pltpu.SemaphoreType.DMA((2,))],
    )
    return pl.pallas_call(
        functools.partial(_combine_kernel, tc=tc, alpha=alpha),
        grid_spec=grid_spec,
        out_shape=[jax.ShapeDtypeStruct((m, d), F32), jax.ShapeDtypeStruct((m, d), BF16)],
        compiler_params=_params(("arbitrary",)),
        name="moe_combine_ln",
    )(dest_flat, y_sorted, h, shared, wts, g, b)


def kernel(x, w_in, w_gate, b_gate, w_branch, w_out, gm_ln_g, gm_ln_b, w_spatial, b_spatial, conv_w,
           ln1_g, ln1_b, w_router, b_router, w_e_gate, w_e_up, w_e_down, w_s_gate, w_s_up, w_s_down,
           ln2_g, ln2_b):
    batch, seq, d = x.shape
    depth = w_in.shape[0]
    m = batch * seq
    bw = d // 4
    heads = bw // HEAD_DIM
    n_exp = w_router.shape[-1]
    alpha = float((2 * depth) ** 0.25)
    tile = EXPERT_ROW_TILE
    n_tiles = (m * TOP_K + n_exp * (tile - 1)) // tile + 1
    n_rows = n_tiles * tile

    b_gate4 = b_gate.reshape(depth, N_BRANCH, 1, d)
    gm_g3 = gm_ln_g.reshape(depth, 1, bw)
    gm_b3 = gm_ln_b.reshape(depth, 1, bw)
    b_sp_bcast = jnp.broadcast_to(b_spatial[..., None], b_spatial.shape + (GM_SPAN,))
    ln1_g3, ln1_b3 = ln1_g.reshape(depth, 1, d), ln1_b.reshape(depth, 1, d)
    ln2_g3, ln2_b3 = ln2_g.reshape(depth, 1, d), ln2_b.reshape(depth, 1, d)
    b_router3 = b_router.reshape(depth, 1, n_exp)

    h = x.reshape(m, d)
    h_bf = h.astype(BF16)
    for l in range(depth):
        qkv = _matmul(h_bf, w_in, l, 0, 3 * bw, *MATMUL_TILE, BF16, "proj_qkv")
        rest = _matmul(h_bf, w_in, l, 3 * bw, 5 * bw, *MATMUL_TILE, F32, "proj_rest")
        o_a = _attention(qkv, batch, seq, heads, *ATTN_TILE)
        o_bc = _gmlp_conv(rest, gm_g3, gm_b3, w_spatial, b_sp_bcast, conv_w, l, seq, bw, TOKEN_TILE)
        merged = _merge(h_bf, o_a, o_bc, w_gate, b_gate4, w_branch, l, *MERGE_TILE)
        y = _matmul(merged, w_out, l, 0, d, *MATMUL_TILE, F32, "out_proj", residual=h, alpha=alpha)
        h, h_bf, h_pk = _residual_ln(y, ln1_g3, ln1_b3, l, TOKEN_TILE)

        idx, rank, wts, counts = _router(h, w_router, b_router3, l, ROUTER_TILE)
        dest2d, tile_expert, n_used, next_used = _route_plan(counts, idx, rank, tile, n_tiles)
        dest = dest2d.reshape(-1)
        act_s, src = _swiglu_and_inverse_map(h_bf, w_s_gate, w_s_up, dest, n_rows, l, *SWIGLU_TILE)
        y_sorted = _experts(tile_expert, n_used, src, next_used, h_pk, w_e_gate, w_e_up, w_e_down, l, n_tiles, tile)
        shared = _matmul(act_s, w_s_down, l, 0, d, *SHARED_DOWN_TILE, F32, "shared_down")
        h, h_bf = _combine(dest, y_sorted, h, shared, wts, ln2_g3, ln2_b3, l, alpha, COMBINE_TILE)
    return h.reshape(batch, seq, d)
```
